```python
import jax, jax.numpy as jnp
from jax import lax
import numpy as np

D_MODEL = 1024
BATCH = 32
SEQ = 2048
DEPTH = 1
DEC_BATCH = 128
DEC_SEQ = 1
PAST_LEN = 8192
PAGE_SIZE = 128

D_CONV = 512
CONV_W = 3
N_HEADS = 8
KV_HEADS = 2
HPG = N_HEADS // KV_HEADS
HEAD_DIM = 64
D_ATTN = N_HEADS * HEAD_DIM
KV_ROW = 2 * KV_HEADS * HEAD_DIM
CMP_STRIDE = 16
CMP_RATIO = 2
CMP_BLK = CMP_STRIDE * CMP_RATIO
CMP_HID = 128
SEL_BLK = 64
TOP_N = 16
WINDOW = 512
N_BRANCH = 3
D_FF = 2816
ROPE_THETA = 10000.0
EPS = 1e-6
Q_BLK = 64
NEG = -1e30
BIG = 1e9
IN_SPLITS = (D_CONV, D_CONV, D_CONV, D_ATTN, KV_ROW, KV_ROW, KV_ROW, N_HEADS * N_BRANCH, 2 * D_MODEL)
N_IN = sum(IN_SPLITS)

kernel_name = 'macaron_shortconv_nsa_hybrid_step'


def rmsnorm(x, g):
    xf = x.astype(jnp.float32)
    y = xf * lax.rsqrt(jnp.mean(xf * xf, axis=-1, keepdims=True) + EPS)
    return (y * g.astype(jnp.float32)).astype(x.dtype)


def half_ffn(x, g, w_gate, w_up, w_down):
    h = rmsnorm(x, g)
    return x + 0.5 * ((jax.nn.silu(h @ w_gate) * (h @ w_up)) @ w_down)


def rope(x, pos):
    half = HEAD_DIM // 2
    freqs = jnp.power(ROPE_THETA, -jnp.arange(half, dtype=jnp.float32) * (2.0 / HEAD_DIM))
    ang = pos.astype(jnp.float32)[:, None] * freqs[None, :]
    cos = jnp.cos(ang)[:, None, :]
    sin = jnp.sin(ang)[:, None, :]
    xf = x.astype(jnp.float32)
    x1, x2 = xf[..., :half], xf[..., half:]
    return jnp.concatenate([x1 * cos - x2 * sin, x1 * sin + x2 * cos], axis=-1).astype(x.dtype)


def mixer_front(h, w_mix_in, pos):
    b_, t_ = h.shape[:2]
    z = h @ w_mix_in
    offs = np.cumsum(IN_SPLITS)[:-1].tolist()
    u_b, u_c, u_x, q, kv_c, kv_s, kv_w, g_n, g_b = jnp.split(z, offs, axis=-1)
    q = rope(q.reshape(b_, t_, N_HEADS, HEAD_DIM), pos).reshape(b_, t_, KV_HEADS, HPG, HEAD_DIM)

    def rows(kv):
        kv = kv.reshape(b_, t_, 2, KV_HEADS, HEAD_DIM)
        return jnp.stack([rope(kv[:, :, 0], pos), kv[:, :, 1]], axis=2)

    g_n = jax.nn.sigmoid(g_n.reshape(b_, t_, KV_HEADS, HPG, N_BRANCH))
    g_b = jax.nn.sigmoid(g_b.reshape(b_, t_, 2, D_MODEL))
    return u_b, u_c * u_x, q, rows(kv_c), rows(kv_s), rows(kv_w), g_n, g_b


def short_conv(u, w, prev):
    ext = jnp.concatenate([prev.astype(u.dtype), u], axis=1)
    t_ = u.shape[1]
    y = w[0] * ext[:, 0:t_]
    for k in range(1, CONV_W):
        y = y + w[k] * ext[:, k:k + t_]
    return y, ext[:, t_:]


def compress(rows, w1, w2, pe):
    b_, l_ = rows.shape[:2]
    n_chunks = l_ // CMP_STRIDE
    n_cmp = n_chunks - CMP_RATIO + 1
    chunks = rows[:, :n_chunks * CMP_STRIDE].reshape(b_, n_chunks, CMP_STRIDE, KV_HEADS, HEAD_DIM)
    part = jnp.einsum('bcsgd,rsdh->bcrgh', chunks, w1)
    hid = jnp.einsum('rsd,rsdh->h', pe.reshape(CMP_RATIO, CMP_STRIDE, HEAD_DIM), w1)
    for r in range(CMP_RATIO):
        hid = hid + part[:, r:r + n_cmp, r]
    return jax.nn.silu(hid) @ w2


def cmp_ends(n_cmp):
    return jnp.arange(n_cmp, dtype=jnp.int32) * CMP_STRIDE + (CMP_BLK - 1)


def cmp_to_sel(n_cmp, n_sel):
    start = jnp.arange(n_cmp, dtype=jnp.int32)[:, None] * CMP_STRIDE
    jb = jnp.arange(n_sel, dtype=jnp.int32)[None, :] * SEL_BLK
    return ((start < jb + SEL_BLK) & (start + CMP_BLK > jb)).astype(jnp.float32)


def key_summaries(rows_cmp, rows_sel, w_cmpk1, w_cmpk2, pe_cmpk, w_cmpv1, w_cmpv2, pe_cmpv):
    kc = compress(rows_cmp[:, :, 0], w_cmpk1, w_cmpk2, pe_cmpk)
    vc = compress(rows_cmp[:, :, 1], w_cmpv1, w_cmpv2, pe_cmpv)
    b_, l_ = rows_sel.shape[:2]
    n_sel = -(-l_ // SEL_BLK)
    padded = jnp.pad(rows_sel, ((0, 0), (0, n_sel * SEL_BLK - l_), (0, 0), (0, 0), (0, 0)))
    blk = padded.reshape(b_, n_sel, SEL_BLK, 2, KV_HEADS, HEAD_DIM).transpose(3, 0, 4, 1, 2, 5)
    return kc, vc, blk[0], blk[1]


def nsa_core(q, t, kc, vc, kb, vb, kw, vw, kw_pos, gates):
    scale = HEAD_DIM ** -0.5
    n_cmp = kc.shape[1]
    n_sel = kb.shape[2]
    s = jnp.einsum('btghd,bngd->btghn', q, kc).astype(jnp.float32) * scale
    m = (cmp_ends(n_cmp)[None, :] <= t[:, None])[None, :, None, None, :]
    p_cmp = jax.nn.softmax(jnp.where(m, s, NEG), axis=-1) * m
    o_cmp = jnp.einsum('btghn,bngd->btghd', p_cmp.astype(vc.dtype), vc)
    imp = jnp.einsum('btghn,nj->btgj', p_cmp, cmp_to_sel(n_cmp, n_sel))
    j = jnp.arange(n_sel, dtype=jnp.int32)[None, :]
    cur = (t // SEL_BLK)[:, None]
    valid = j <= cur
    forced = valid & ((j == 0) | (j == cur) | (j == cur - 1))
    score = jnp.where(forced[None, :, None, :], BIG, jnp.where(valid[None, :, None, :], imp, -BIG))
    _, idx = lax.top_k(score, min(TOP_N, n_sel))
    bi = jnp.arange(q.shape[0])[:, None, None, None]
    gi = jnp.arange(KV_HEADS)[None, None, :, None]
    kg = kb[bi, gi, idx]
    vg = vb[bi, gi, idx]
    kpos = idx[..., None] * SEL_BLK + jnp.arange(SEL_BLK, dtype=jnp.int32)
    ms = (kpos <= t[None, :, None, None, None])[:, :, :, None]
    s = jnp.einsum('btghd,btgnkd->btghnk', q, kg).astype(jnp.float32) * scale
    s = jnp.where(ms, s, NEG)
    sh = s.shape
    p = jax.nn.softmax(s.reshape(sh[:-2] + (sh[-2] * sh[-1],)), axis=-1).reshape(sh)
    o_sel = jnp.einsum('btghnk,btgnkd->btghd', p.astype(vg.dtype), vg)
    s = jnp.einsum('btghd,bkgd->btghk', q, kw).astype(jnp.float32) * scale
    mw = (kw_pos[None, :] <= t[:, None]) & (kw_pos[None, :] > t[:, None] - WINDOW) & (kw_pos[None, :] >= 0)
    p = jax.nn.softmax(jnp.where(mw[None, :, None, None, :], s, NEG), axis=-1)
    o_win = jnp.einsum('btghk,bkgd->btghd', p.astype(vw.dtype), vw)
    return gates[..., 0:1] * o_cmp + gates[..., 1:2] * o_sel + gates[..., 2:3] * o_win


def mixer_merge(u_b, conv_y, o_attn, g_b, w_conv_out, w_attn_out, w_mix_out):
    y_c = (u_b * conv_y) @ w_conv_out
    y_a = o_attn @ w_attn_out
    return (g_b[:, :, 0] * y_c + g_b[:, :, 1] * y_a) @ w_mix_out


def mixer_prompt(h, w_mix_in, conv_w, w_cmpk1, w_cmpk2, pe_cmpk, w_cmpv1, w_cmpv2, pe_cmpv,
                 w_conv_out, w_attn_out, w_mix_out):
    b_, t_ = h.shape[:2]
    pos = jnp.arange(t_, dtype=jnp.int32)
    u_b, u_in, q, r_cmp, r_sel, r_win, g_n, g_b = mixer_front(h, w_mix_in, pos)
    conv_y, conv_state = short_conv(u_in, conv_w, jnp.zeros((b_, CONV_W - 1, D_CONV), h.dtype))
    kc, vc, kb, vb = key_summaries(r_cmp, r_sel, w_cmpk1, w_cmpk2, pe_cmpk, w_cmpv1, w_cmpv2, pe_cmpv)
    win_pad = jnp.pad(r_win, ((0, 0), (WINDOW, 0), (0, 0), (0, 0), (0, 0)))
    n_qb = t_ // Q_BLK
    qb = q.reshape(b_, n_qb, Q_BLK, KV_HEADS, HPG, HEAD_DIM).swapaxes(0, 1)
    gb = g_n.reshape(b_, n_qb, Q_BLK, KV_HEADS, HPG, N_BRANCH).swapaxes(0, 1)

    def block(args):
        q_i, g_i, i = args
        start = i * Q_BLK
        t = start + jnp.arange(Q_BLK, dtype=jnp.int32)
        w = lax.dynamic_slice_in_dim(win_pad, start, Q_BLK + WINDOW, axis=1)
        w_pos = start - WINDOW + jnp.arange(Q_BLK + WINDOW, dtype=jnp.int32)
        return nsa_core(q_i, t, kc, vc, kb, vb, w[:, :, 0], w[:, :, 1], w_pos, g_i)

    o = lax.map(block, (qb, gb, jnp.arange(n_qb, dtype=jnp.int32)))
    o = o.swapaxes(0, 1).reshape(b_, t_, D_ATTN)
    out = mixer_merge(u_b, conv_y, o, g_b, w_conv_out, w_attn_out, w_mix_out)
    return out, r_cmp, r_sel, r_win[:, t_ - min(WINDOW, t_):], conv_state


def mixer_sample(h, cache_cmp, cache_sel, win_buf, conv_buf, page_table, w_mix_in, conv_w,
                 w_cmpk1, w_cmpk2, pe_cmpk, w_cmpv1, w_cmpv2, pe_cmpv, w_conv_out, w_attn_out, w_mix_out):
    b_, s_ = h.shape[:2]
    pos = PAST_LEN + jnp.arange(s_, dtype=jnp.int32)
    u_b, u_in, q, r_cmp, r_sel, r_win, g_n, g_b = mixer_front(h, w_mix_in, pos)
    conv_y, conv_state = short_conv(u_in, conv_w, conv_buf)
    past_cmp = cache_cmp[page_table].reshape(b_, -1, 2, KV_HEADS, HEAD_DIM)
    past_sel = cache_sel[page_table].reshape(b_, -1, 2, KV_HEADS, HEAD_DIM)
    kc, vc, kb, vb = key_summaries(jnp.concatenate([past_cmp, r_cmp.astype(past_cmp.dtype)], axis=1),
                                   jnp.concatenate([past_sel, r_sel.astype(past_sel.dtype)], axis=1),
                                   w_cmpk1, w_cmpk2, pe_cmpk, w_cmpv1, w_cmpv2, pe_cmpv)
    w_all = jnp.concatenate([win_buf, r_win.astype(win_buf.dtype)], axis=1)
    w_buf = win_buf.shape[1]
    w_pos = PAST_LEN - w_buf + jnp.arange(w_buf + s_, dtype=jnp.int32)
    o = nsa_core(q, pos, kc, vc, kb, vb, w_all[:, :, 0], w_all[:, :, 1], w_pos, g_n).reshape(b_, s_, D_ATTN)
    out = mixer_merge(u_b, conv_y, o, g_b, w_conv_out, w_attn_out, w_mix_out)
    return out, r_cmp, r_sel, w_all[:, s_:], conv_state


def setup_inputs(seed: int = 0) -> dict:
    key = jax.random.key(seed)
    k = jax.random.split(key, 32)

    def nrm(i, shape, scale):
        return jax.random.normal(k[i], shape, jnp.float32) * scale

    n_pages = PAST_LEN // PAGE_SIZE
    n_used = DEC_BATCH * n_pages
    n_phys = n_used + (n_used + 3) // 4
    w_buf = min(WINDOW, PAST_LEN)
    L = DEPTH
    page_table = jax.random.permutation(k[2], n_phys)[:n_used].reshape(DEC_BATCH, n_pages).astype(jnp.int32)
    return {
        'x_prompt': nrm(0, (BATCH, SEQ, D_MODEL), 1.0),
        'x_sample': nrm(1, (DEC_BATCH, DEC_SEQ, D_MODEL), 1.0),
        'cache_cmp_kv': nrm(3, (L, n_phys, PAGE_SIZE, 2, KV_HEADS, HEAD_DIM), 1.0),
        'cache_sel_kv': nrm(4, (L, n_phys, PAGE_SIZE, 2, KV_HEADS, HEAD_DIM), 1.0),
        'state_win_kv': nrm(5, (L, DEC_BATCH, w_buf, 2, KV_HEADS, HEAD_DIM), 1.0),
        'state_conv': nrm(6, (L, DEC_BATCH, CONV_W - 1, D_CONV), 1.0),
        'page_table': page_table,
        'norm_ffn1': 1.0 + nrm(7, (L, D_MODEL), 0.05),
        'ffn1_w_gate': nrm(8, (L, D_MODEL, D_FF), D_MODEL ** -0.5),
        'ffn1_w_up': nrm(9, (L, D_MODEL, D_FF), D_MODEL ** -0.5),
        'ffn1_w_down': nrm(10, (L, D_FF, D_MODEL), D_FF ** -0.5),
        'norm_mix': 1.0 + nrm(11, (L, D_MODEL), 0.05),
        'w_mix_in': nrm(12, (L, D_MODEL, N_IN), D_MODEL ** -0.5),
        'conv_w': nrm(13, (L, CONV_W, D_CONV), CONV_W ** -0.5),
        'w_cmpk1': nrm(14, (L, CMP_RATIO, CMP_STRIDE, HEAD_DIM, CMP_HID), (CMP_BLK * HEAD_DIM) ** -0.5),
        'w_cmpk2': nrm(15, (L, CMP_HID, HEAD_DIM), CMP_HID ** -0.5),
        'pe_cmpk': nrm(16, (L, CMP_BLK, HEAD_DIM), 0.1),
        'w_cmpv1': nrm(17, (L, CMP_RATIO, CMP_STRIDE, HEAD_DIM, CMP_HID), (CMP_BLK * HEAD_DIM) ** -0.5),
        'w_cmpv2': nrm(18, (L, CMP_HID, HEAD_DIM), CMP_HID ** -0.5),
        'pe_cmpv': nrm(19, (L, CMP_BLK, HEAD_DIM), 0.1),
        'w_conv_out': nrm(20, (L, D_CONV, D_MODEL), D_CONV ** -0.5),
        'w_attn_out': nrm(21, (L, D_ATTN, D_MODEL), D_ATTN ** -0.5),
        'w_mix_out': nrm(22, (L, D_MODEL, D_MODEL), D_MODEL ** -0.5),
        'norm_ffn2': 1.0 + nrm(23, (L, D_MODEL), 0.05),
        'ffn2_w_gate': nrm(24, (L, D_MODEL, D_FF), D_MODEL ** -0.5),
        'ffn2_w_up': nrm(25, (L, D_MODEL, D_FF), D_MODEL ** -0.5),
        'ffn2_w_down': nrm(26, (L, D_FF, D_MODEL), D_FF ** -0.5),
        'norm_final': 1.0 + nrm(27, (D_MODEL,), 0.05),
    }


def reference(x_prompt, x_sample, cache_cmp_kv, cache_sel_kv, state_win_kv, state_conv, page_table,
              norm_ffn1, ffn1_w_gate, ffn1_w_up, ffn1_w_down, norm_mix, w_mix_in, conv_w,
              w_cmpk1, w_cmpk2, pe_cmpk, w_cmpv1, w_cmpv2, pe_cmpv, w_conv_out, w_attn_out, w_mix_out,
              norm_ffn2, ffn2_w_gate, ffn2_w_up, ffn2_w_down, norm_final):
    xp, xs = x_prompt, x_sample
    cmp_p, sel_p, win_p, conv_p = [], [], [], []
    cmp_s, sel_s, win_s, conv_s = [], [], [], []
    for l in range(DEPTH):
        mw = (w_mix_in[l], conv_w[l], w_cmpk1[l], w_cmpk2[l], pe_cmpk[l], w_cmpv1[l], w_cmpv2[l], pe_cmpv[l],
              w_conv_out[l], w_attn_out[l], w_mix_out[l])
        xp = half_ffn(xp, norm_ffn1[l], ffn1_w_gate[l], ffn1_w_up[l], ffn1_w_down[l])
        xs = half_ffn(xs, norm_ffn1[l], ffn1_w_gate[l], ffn1_w_up[l], ffn1_w_down[l])
        mp, rc, rs, rw, rv = mixer_prompt(rmsnorm(xp, norm_mix[l]), *mw)
        cmp_p.append(rc); sel_p.append(rs); win_p.append(rw); conv_p.append(rv)
        ms, rc, rs, rw, rv = mixer_sample(rmsnorm(xs, norm_mix[l]), cache_cmp_kv[l], cache_sel_kv[l],
                                          state_win_kv[l], state_conv[l], page_table, *mw)
        cmp_s.append(rc); sel_s.append(rs); win_s.append(rw); conv_s.append(rv)
        xp = xp + mp
        xs = xs + ms
        xp = half_ffn(xp, norm_ffn2[l], ffn2_w_gate[l], ffn2_w_up[l], ffn2_w_down[l])
        xs = half_ffn(xs, norm_ffn2[l], ffn2_w_gate[l], ffn2_w_up[l], ffn2_w_down[l])
    y_prompt = rmsnorm(xp, norm_final)
    y_sample = rmsnorm(xs, norm_final)
    return (y_prompt, y_sample, jnp.stack(cmp_p), jnp.stack(sel_p), jnp.stack(win_p), jnp.stack(conv_p),
            jnp.stack(cmp_s), jnp.stack(sel_s), jnp.stack(win_s), jnp.stack(conv_s))
```

```python
import functools

import numpy as np
import jax
import jax.numpy as jnp
from jax import lax
from jax.experimental import pallas as pl
from jax.experimental.pallas import tpu as pltpu

F32 = jnp.float32
BF16 = jnp.bfloat16

D_MODEL = 1024
D_CONV = 512
CONV_W = 3
N_HEADS = 8
KV_HEADS = 2
HPG = N_HEADS // KV_HEADS
HEAD_DIM = 64
D_ATTN = N_HEADS * HEAD_DIM
KV_ROW = 2 * KV_HEADS * HEAD_DIM
CMP_STRIDE = 16
CMP_RATIO = 2
CMP_BLK = CMP_STRIDE * CMP_RATIO
CMP_HID = 128
SEL_BLK = 64
TOP_N = 16
WINDOW = 512
N_BRANCH = 3
D_FF = 2816
ROPE_THETA = 10000.0
EPS = 1e-6
NEG = -1e30
BIG = 1e9
GONE = -3e38
IN_SPLITS = (D_CONV, D_CONV, D_CONV, D_ATTN, KV_ROW, KV_ROW, KV_ROW, N_HEADS * N_BRANCH, 2 * D_MODEL)

LANES = 128
KV_LANES = KV_HEADS * HEAD_DIM
assert KV_LANES == LANES and KV_HEADS == 2 and 2 * HEAD_DIM == LANES
HEAD_ORDER = tuple(half * HPG + v for v in range(HPG) for half in range(KV_HEADS))
VMEM_LIMIT = 56 * 1024 * 1024


def _cparams(n_grid):
    return pltpu.CompilerParams(dimension_semantics=("arbitrary",) * n_grid, vmem_limit_bytes=VMEM_LIMIT)


def _const_spec(shape):
    nd = len(shape)
    return pl.BlockSpec(shape, lambda *_: (0,) * nd, pipeline_mode=pl.Buffered(1))


def _rms(x, g):
    return x * lax.rsqrt(jnp.mean(x * x, axis=-1, keepdims=True) + EPS) * g


def _silu(x):
    return x * jax.nn.sigmoid(x)


def _div(x, n):
    return lax.shift_right_logical(x, int(np.log2(n)))


def _mod(x, n):
    return x & (n - 1)


def _dot(a, b):
    return jnp.dot(a, b, preferred_element_type=F32)


def _dot_nt(a, b):
    return lax.dot_general(a, b, (((1,), (1,)), ((), ())), preferred_element_type=F32)


def _ffn_body(x_ref, g_ref, wg_ref, wu_ref, wd_ref, gf_ref, o_ref, *, final_norm):
    x = x_ref[...]
    h = _rms(x, g_ref[...]).astype(BF16)
    a = (_silu(_dot(h, wg_ref[...])) * _dot(h, wu_ref[...])).astype(BF16)
    y = x + 0.5 * _dot(a, wd_ref[...])
    if final_norm:
        y = _rms(y, gf_ref[...])
    o_ref[...] = y


def _ffn(x, g, wg, wu, wd, gf, *, final_norm, tm):
    m = x.shape[0]
    row = pl.BlockSpec((tm, D_MODEL), lambda i: (i, 0))
    return pl.pallas_call(
        functools.partial(_ffn_body, final_norm=final_norm),
        grid=(m // tm,),
        in_specs=[row, _const_spec((1, D_MODEL)), _const_spec(wg.shape), _const_spec(wu.shape),
                  _const_spec(wd.shape), _const_spec((1, D_MODEL))],
        out_specs=row,
        out_shape=jax.ShapeDtypeStruct((m, D_MODEL), F32),
        compiler_params=_cparams(1),
        name="ffn",
    )(x, g, wg, wu, wd, gf)


MIXIN_COLS = (("uc", D_CONV), ("ux", D_CONV), ("q", D_ATTN), ("kvc", KV_ROW), ("kvs", KV_ROW), ("kvw", KV_ROW),
              ("gn", LANES))
MIXIN_OFF = {}
_o = 0
for _n, _w in MIXIN_COLS:
    MIXIN_OFF[_n] = (_o, _o + _w)
    _o += _w
MIXIN_N = _o


def _rope_tile(z, cos, sin):
    first = _mod(lax.broadcasted_iota(jnp.int32, z.shape, 1), HEAD_DIM) < (HEAD_DIM // 2)
    rot = jnp.where(first, pltpu.roll(z, LANES - HEAD_DIM // 2, axis=1), pltpu.roll(z, HEAD_DIM // 2, axis=1))
    return z * cos + rot * sin


def _mixin_body(x_ref, g_ref, w_ref, cos_ref, sin_ref,
                uin_ref, q_ref, rc_ref, rs_ref, rw_ref, ks_ref, kw_ref, gn_ref):
    h = _rms(x_ref[...], g_ref[...]).astype(BF16)
    cos = cos_ref[...]
    sin = sin_ref[...]

    def proj(name):
        lo, hi = MIXIN_OFF[name]
        return _dot(h, w_ref[:, lo:hi])

    uin_ref[...] = proj("uc") * proj("ux")
    zq = proj("q")
    scale = HEAD_DIM ** -0.5
    q_ref[...] = jnp.concatenate(
        [_rope_tile(zq[:, v * LANES:(v + 1) * LANES], cos, sin) * scale for v in range(D_ATTN // LANES)],
        axis=1).astype(BF16)
    for name, f_ref, b_ref in (("kvc", rc_ref, None), ("kvs", rs_ref, ks_ref), ("kvw", rw_ref, kw_ref)):
        z = proj(name)
        rows = jnp.concatenate([_rope_tile(z[:, :KV_LANES], cos, sin), z[:, KV_LANES:]], axis=1)
        f_ref[...] = rows
        if b_ref is not None:
            b_ref[...] = rows.astype(BF16)
    gn_ref[...] = jax.nn.sigmoid(proj("gn"))


def _mixin(x, g, w, cos, sin, *, tm, pos_blocks):
    m = x.shape[0]
    row = lambda n: pl.BlockSpec((tm, n), lambda i: (i, 0))
    tab = pl.BlockSpec((tm, LANES), lambda i: (i % pos_blocks, 0))
    sds = lambda n, dt: jax.ShapeDtypeStruct((m, n), dt)
    return pl.pallas_call(
        _mixin_body,
        grid=(m // tm,),
        in_specs=[row(D_MODEL), _const_spec((1, D_MODEL)), _const_spec(w.shape), tab, tab],
        out_specs=[row(D_CONV), row(D_ATTN), row(KV_ROW), row(KV_ROW), row(KV_ROW), row(KV_ROW), row(KV_ROW),
                   row(LANES)],
        out_shape=[sds(D_CONV, F32), sds(D_ATTN, BF16), sds(KV_ROW, F32), sds(KV_ROW, F32), sds(KV_ROW, F32),
                   sds(KV_ROW, BF16), sds(KV_ROW, BF16), sds(LANES, F32)],
        compiler_params=_cparams(1),
        name="mix_in",
    )(x, g, w, cos, sin)


def _compress(load_rows, n_chunks, pe0_ref, pe1_ref, wbd_ref, w2k_ref, w2v_ref):
    half = KV_ROW * CMP_RATIO
    acc0 = jnp.zeros((n_chunks, half), F32)
    acc1 = jnp.zeros((n_chunks, half), F32)
    for s in range(CMP_STRIDE):
        a = load_rows(s)
        acc0 = acc0 + _dot((a + pe0_ref[s:s + 1, :]).astype(BF16), wbd_ref[s, :, :half])
        acc1 = acc1 + _dot((a + pe1_ref[s:s + 1, :]).astype(BF16), wbd_ref[s, :, half:])
    hid = acc0 + pltpu.roll(acc1, n_chunks - 1, axis=0)
    act = _silu(hid).astype(BF16)
    keep = lax.broadcasted_iota(jnp.int32, (n_chunks, LANES), 0) < n_chunks - 1
    kc = jnp.where(keep, _dot(act[:, :half // 2], w2k_ref[...]), 0.0)
    vc = jnp.where(keep, _dot(act[:, half // 2:], w2v_ref[...]), 0.0)
    return kc, vc


def _load_stride_rows(ref, s, n_chunks):
    parts = [ref[pl.ds(KV_ROW // LANES * s + c, n_chunks, stride=KV_ROW // LANES * CMP_STRIDE), :]
             for c in range(KV_ROW // LANES)]
    return jnp.concatenate(parts, axis=1)


def _compress_prompt_body(x_ref, pe0_ref, pe1_ref, wbd_ref, w2k_ref, w2v_ref, kc_ref, vc_ref, *, n_chunks):
    kc, vc = _compress(lambda s: _load_stride_rows(x_ref.at[0], s, n_chunks), n_chunks,
                       pe0_ref, pe1_ref, wbd_ref, w2k_ref, w2v_ref)
    kc_ref[0] = kc.astype(BF16)
    vc_ref[0] = vc.astype(BF16)


def _compress_prompt(r_cmp, cw):
    b, t, _ = r_cmp.shape
    n_chunks = t // CMP_STRIDE
    out = pl.BlockSpec((1, n_chunks, LANES), lambda i: (i, 0, 0))
    return pl.pallas_call(
        functools.partial(_compress_prompt_body, n_chunks=n_chunks),
        grid=(b,),
        in_specs=[pl.BlockSpec((1, t * KV_ROW // LANES, LANES), lambda i: (i, 0, 0))] +
                 [_const_spec(w.shape) for w in cw],
        out_specs=[out, out],
        out_shape=[jax.ShapeDtypeStruct((b, n_chunks, LANES), BF16)] * 2,
        compiler_params=_cparams(1),
        name="compress_prompt",
    )(r_cmp.reshape(b, t * KV_ROW // LANES, LANES), *cw)


def _softmax_rows(s):
    e = jnp.exp(s - jnp.max(s, axis=-1, keepdims=True))
    return e / jnp.sum(e, axis=-1, keepdims=True)


def _split_hi_lo(x):
    hi = x.astype(BF16)
    return hi, (x - hi.astype(F32)).astype(BF16)


def _nsa_prompt_body(q_ref, gn_ref, kc_ref, vc_ref, ks_ref, kw_ref, e_ref, c2s_ref, o_ref,
                     qaug_ref, m_ref, l_ref, acc_ref, out_ref, *, tq, kt, t_len):
    n_slots = N_HEADS
    rows = n_slots * tq
    n_sel = t_len // SEL_BLK
    start = pl.program_id(1) * tq
    lane = lax.broadcasted_iota(jnp.int32, (tq, LANES), 1)
    t_pos = start + lax.broadcasted_iota(jnp.int32, (tq, LANES), 0)
    low = lane < HEAD_DIM
    gn = gn_ref[0]

    def gate(slot, branch):
        v, half = divmod(slot, KV_HEADS)
        c = (half * HPG + v) * N_BRANCH + branch
        return gn[:, c:c + 1]

    q = q_ref[0].astype(F32)
    kc = kc_ref[0]
    vc = vc_ref[0]
    cmp_ok = lane * CMP_STRIDE + (CMP_BLK - 1) <= t_pos
    psum = [jnp.zeros((tq, LANES), F32) for _ in range(KV_HEADS)]
    for slot in range(n_slots):
        v, half = divmod(slot, KV_HEADS)
        qv = q[:, v * LANES:(v + 1) * LANES]
        qe = (jnp.where(low, qv, 0.0) if half == 0 else jnp.where(low, 0.0, qv)).astype(BF16)
        qaug_ref[slot * tq:(slot + 1) * tq, :LANES] = qe
        p = jnp.where(cmp_ok, _softmax_rows(jnp.where(cmp_ok, _dot_nt(qe, kc), NEG)), 0.0)
        psum[half] = psum[half] + p
        out_ref[slot * tq:(slot + 1) * tq, :] = gate(slot, 0) * _dot(p.astype(BF16), vc)
    cur = _div(t_pos, SEL_BLK)
    valid = lane <= cur
    forced = valid & ((lane == 0) | (lane == cur) | (lane == cur - 1))
    pens = []
    for half in range(KV_HEADS):
        hi, lo = _split_hi_lo(psum[half])
        imp = _dot(hi, c2s_ref[...]) + _dot(lo, c2s_ref[...])
        score = jnp.where(forced, BIG, jnp.where(valid, imp, -BIG))
        rank = jnp.zeros((tq, LANES), F32)
        for i in range(n_sel):
            col = score[:, i:i + 1]
            tie = jnp.where(lane > i, 1.0, 0.0)
            rank = rank + jnp.where(col > score, 1.0, jnp.where(col == score, tie, 0.0))
        pens.append(jnp.where(lane < n_sel, jnp.where(rank >= TOP_N, NEG, 0.0), 0.0).astype(BF16))
    for slot in range(n_slots):
        qaug_ref[slot * tq:(slot + 1) * tq, LANES:] = pens[slot % KV_HEADS]

    d = lax.broadcasted_iota(jnp.int32, (tq, kt), 1) - lax.broadcasted_iota(jnp.int32, (tq, kt), 0)

    def reset():
        m_ref[...] = jnp.full((rows, LANES), NEG, F32)
        l_ref[...] = jnp.zeros((rows, LANES), F32)
        acc_ref[...] = jnp.zeros((rows, LANES), F32)

    def tile_step(kv_ref, koff, select, window):
        kv = kv_ref[0, pl.ds(koff, kt), :]
        k = kv[:, :KV_LANES]
        v = kv[:, KV_LANES:]
        if select:
            s_all = _dot_nt(qaug_ref[...], jnp.concatenate([k, e_ref[pl.ds(koff, kt), :]], axis=1))
        else:
            s_all = _dot_nt(qaug_ref[:, :LANES], k)
        rel = d + (koff - start)
        ok = rel <= 0
        if window:
            ok = ok & (rel > -WINDOW)
        for slot in range(n_slots):
            rs = slice(slot * tq, (slot + 1) * tq)
            s = jnp.where(ok, s_all[rs], NEG)
            m_prev = m_ref[rs, :]
            m_new = jnp.maximum(m_prev, jnp.max(s, axis=-1, keepdims=True))
            alpha = jnp.exp(m_prev - m_new)
            p = jnp.exp(s - jnp.concatenate([m_new] * (kt // LANES), axis=1))
            l_ref[rs, :] = alpha * l_ref[rs, :] + jnp.sum(p, axis=-1, keepdims=True)
            acc_ref[rs, :] = alpha * acc_ref[rs, :] + _dot(p.astype(BF16), v)
            m_ref[rs, :] = m_new

    def finish(branch):
        for slot in range(n_slots):
            rs = slice(slot * tq, (slot + 1) * tq)
            out_ref[rs, :] = out_ref[rs, :] + gate(slot, branch) * (acc_ref[rs, :] / l_ref[rs, :])

    reset()

    def sel_body(j, carry):
        tile_step(ks_ref, pl.multiple_of(j * kt, kt), True, False)
        return carry

    lax.fori_loop(0, (start + tq + kt - 1) // kt, sel_body, 0)
    finish(1)

    reset()
    first_tile = lax.shift_right_arithmetic(start - WINDOW, int(np.log2(kt)))
    for jw in range((WINDOW + tq + kt - 1) // kt + 1):
        ti = first_tile + jw

        @pl.when((ti >= 0) & (ti * kt < start + tq))
        def _():
            tile_step(kw_ref, pl.multiple_of(ti * kt, kt), False, True)
    finish(2)

    o_ref[0] = jnp.concatenate(
        [jnp.where(low, out_ref[(2 * v) * tq:(2 * v + 1) * tq, :], out_ref[(2 * v + 1) * tq:(2 * v + 2) * tq, :])
         for v in range(HPG)], axis=1).astype(BF16)


def _nsa_prompt(q, gn, kc, vc, ks, kw, e_mat, c2s, *, tq, kt):
    b, t, _ = q.shape
    n_cmp = kc.shape[1]
    rows = N_HEADS * tq
    tile = lambda n: pl.BlockSpec((1, tq, n), lambda i, j: (i, j, 0))
    per_b = lambda r, n: pl.BlockSpec((1, r, n), lambda i, j: (i, 0, 0))
    return pl.pallas_call(
        functools.partial(_nsa_prompt_body, tq=tq, kt=kt, t_len=t),
        grid=(b, t // tq),
        in_specs=[tile(D_ATTN), tile(LANES), per_b(n_cmp, LANES), per_b(n_cmp, LANES), per_b(t, KV_ROW),
                  per_b(t, KV_ROW), _const_spec(e_mat.shape), _const_spec(c2s.shape)],
        out_specs=tile(D_ATTN),
        out_shape=jax.ShapeDtypeStruct((b, t, D_ATTN), BF16),
        scratch_shapes=[pltpu.VMEM((rows, 2 * LANES), BF16), pltpu.VMEM((rows, LANES), F32),
                        pltpu.VMEM((rows, LANES), F32), pltpu.VMEM((rows, LANES), F32),
                        pltpu.VMEM((rows, LANES), F32)],
        compiler_params=_cparams(2),
        name="nsa_prompt",
    )(q, gn, kc, vc, ks, kw, e_mat, c2s)


def _merge_body(x_ref, g_ref, u_ref, um1_ref, um2_ref, o_ref, wb_ref, cw_ref, wco_ref, wao_ref, wmo_ref,
                y_ref, *, tm, tiles_per_seq):
    x = x_ref[...]
    h = _rms(x, g_ref[...]).astype(BF16)
    ub = _dot(h, wb_ref[:, :D_CONV])
    u = u_ref[...]
    if tiles_per_seq is None:
        um1 = um1_ref[...]
        um2 = um2_ref[...]
    else:
        row = lax.broadcasted_iota(jnp.int32, (tm, D_CONV), 0)
        keep = jnp.where(pl.program_id(0) % tiles_per_seq == 0, 0.0, 1.0)
        prev = um1_ref[...] * keep
        um1 = jnp.where(row == 0, prev[7:8, :], pltpu.roll(u, 1, axis=0))
        um2 = jnp.where(row == 0, prev[6:7, :], jnp.where(row == 1, prev[7:8, :], pltpu.roll(u, 2, axis=0)))
    conv = cw_ref[0:1, :] * um2 + cw_ref[1:2, :] * um1 + cw_ref[2:3, :] * u
    y_c = _dot((ub * conv).astype(BF16), wco_ref[...])
    y_a = _dot(o_ref[...], wao_ref[...])
    g_c = jax.nn.sigmoid(_dot(h, wb_ref[:, D_CONV:D_CONV + D_MODEL]))
    g_a = jax.nn.sigmoid(_dot(h, wb_ref[:, D_CONV + D_MODEL:]))
    y_ref[...] = x + _dot((g_c * y_c + g_a * y_a).astype(BF16), wmo_ref[...])


def _merge(x, g, uin, um1, um2, o, wb, cw, wco, wao, wmo, *, tm, tiles_per_seq):
    m = x.shape[0]
    row = lambda n: pl.BlockSpec((tm, n), lambda i: (i, 0))
    if tiles_per_seq is None:
        prev_specs = [row(D_CONV), row(D_CONV)]
    else:
        halo = pl.BlockSpec((8, D_CONV), lambda i: (jnp.maximum(i * (tm // 8) - 1, 0), 0))
        prev_specs = [halo, halo]
    return pl.pallas_call(
        functools.partial(_merge_body, tm=tm, tiles_per_seq=tiles_per_seq),
        grid=(m // tm,),
        in_specs=[row(D_MODEL), _const_spec((1, D_MODEL)), row(D_CONV)] + prev_specs +
                 [row(D_ATTN), _const_spec(wb.shape), _const_spec(cw.shape), _const_spec(wco.shape),
                  _const_spec(wao.shape), _const_spec(wmo.shape)],
        out_specs=row(D_MODEL),
        out_shape=jax.ShapeDtypeStruct((m, D_MODEL), F32),
        compiler_params=_cparams(1),
        name="merge",
    )(x, g, uin, um1, um2, o, wb, cw, wco, wao, wmo)


def _sample_cmp_body(pt_ref, cache_ref, q_ref, pe0_ref, pe1_ref, wbd_ref, w2k_ref, w2v_ref, c2s_ref,
                     ocmp_ref, idx_ref, buf_ref, sem_ref, *, n_pages, page, t_pos):
    b = pl.program_id(0)
    n_b = pl.num_programs(0)
    n_chunks = n_pages * page // CMP_STRIDE
    n_sel = t_pos // SEL_BLK + 1

    page_rows = page * KV_ROW // LANES

    def page_copy(bb, p, slot):
        return pltpu.make_async_copy(cache_ref.at[pt_ref[bb * n_pages + p]],
                                     buf_ref.at[slot, pl.ds(p * page_rows, page_rows), :], sem_ref.at[slot])

    def start_all(bb, slot):
        for p in range(n_pages):
            page_copy(bb, p, slot).start()

    @pl.when(b == 0)
    def _():
        start_all(0, 0)

    @pl.when(b + 1 < n_b)
    def _():
        start_all(b + 1, (b + 1) % 2)

    slot = b % 2
    for p in range(n_pages):
        page_copy(b, p, slot).wait()

    kc, vc = _compress(lambda s: _load_stride_rows(buf_ref.at[slot], s, n_chunks), n_chunks,
                       pe0_ref, pe1_ref, wbd_ref, w2k_ref, w2v_ref)

    row = lax.broadcasted_iota(jnp.int32, (N_HEADS, LANES), 0)
    lane = lax.broadcasted_iota(jnp.int32, (N_HEADS, LANES), 1)
    q = q_ref[0].astype(F32)
    qe = jnp.zeros((N_HEADS, LANES), F32)
    for v in range(HPG):
        qv = jnp.broadcast_to(q[:, v * LANES:(v + 1) * LANES], (N_HEADS, LANES))
        qe = jnp.where((_div(row, KV_HEADS) == v) & (_div(lane, HEAD_DIM) == _mod(row, KV_HEADS)), qv, qe)
    cidx = lax.broadcasted_iota(jnp.int32, (N_HEADS, n_chunks), 1)
    cmp_ok = cidx * CMP_STRIDE + (CMP_BLK - 1) <= t_pos
    p = jnp.where(cmp_ok, _softmax_rows(jnp.where(cmp_ok, _dot_nt(qe.astype(BF16), kc.astype(BF16)), NEG)), 0.0)
    ocmp_ref[0] = _dot(p.astype(BF16), vc.astype(BF16))

    hi, lo = _split_hi_lo(p)
    imp_h = _dot(hi, c2s_ref[...]) + _dot(lo, c2s_ref[...])
    n_lanes = c2s_ref.shape[1]
    grow = lax.broadcasted_iota(jnp.int32, (N_HEADS, n_lanes), 0)
    glane = lax.broadcasted_iota(jnp.int32, (N_HEADS, n_lanes), 1)
    imp = jnp.zeros((N_HEADS, n_lanes), F32)
    for g in range(KV_HEADS):
        tot = jnp.sum(jnp.where(_mod(grow, KV_HEADS) == g, imp_h, 0.0), axis=0, keepdims=True)
        imp = jnp.where(grow == g, jnp.broadcast_to(tot, imp.shape), imp)
    cur = t_pos // SEL_BLK
    forced = (glane == 0) | (glane == cur) | (glane == cur - 1)
    score = jnp.where(glane < n_sel, jnp.where(forced, BIG, imp), GONE)
    flane = glane.astype(F32)
    picked = jnp.zeros((N_HEADS, LANES), jnp.int32)
    for k in range(TOP_N):
        best = jnp.max(score, axis=-1, keepdims=True)
        where_best = jnp.min(jnp.where(score == best, flane, float(n_lanes)), axis=-1, keepdims=True)
        picked = jnp.where(lane == k, where_best.astype(jnp.int32), picked)
        score = jnp.where(flane == where_best, GONE, score)
    idx_ref[0] = picked


def _sample_cmp(page_table, cache, q, cw, c2s, *, t_pos):
    bsz, n_pages = page_table.shape
    _, page, _ = cache.shape
    n_rows = n_pages * page
    gs = pltpu.PrefetchScalarGridSpec(
        num_scalar_prefetch=1,
        grid=(bsz,),
        in_specs=[pl.BlockSpec(memory_space=pl.ANY), pl.BlockSpec((1, 1, D_ATTN), lambda i, pt: (i, 0, 0))] +
                 [pl.BlockSpec(w.shape, (lambda nd: lambda i, pt: (0,) * nd)(w.ndim), pipeline_mode=pl.Buffered(1))
                  for w in (*cw, c2s)],
        out_specs=[pl.BlockSpec((1, N_HEADS, LANES), lambda i, pt: (i, 0, 0))] * 2,
        scratch_shapes=[pltpu.VMEM((2, n_rows * KV_ROW // LANES, LANES), F32), pltpu.SemaphoreType.DMA((2,))],
    )
    cache = cache.reshape(-1, page * KV_ROW // LANES, LANES)
    return pl.pallas_call(
        functools.partial(_sample_cmp_body, n_pages=n_pages, page=page, t_pos=t_pos),
        grid_spec=gs,
        out_shape=[jax.ShapeDtypeStruct((bsz, N_HEADS, LANES), F32),
                   jax.ShapeDtypeStruct((bsz, N_HEADS, LANES), jnp.int32)],
        compiler_params=_cparams(1),
        name="sample_cmp",
    )(page_table.reshape(-1), cache, q, *cw, c2s)


def _sample_attn_body(pt_ref, idx_ref, cache_ref, q_ref, gn_ref, ns_ref, nw_ref, win_ref, ocmp_ref,
                      o_ref, wout_ref, buf_ref, sem_ref, *, n_pages, page, t_pos):
    b = pl.program_id(0)
    n_b = pl.num_programs(0)
    halves = page // SEL_BLK
    new_blk = t_pos // SEL_BLK
    n_slot = KV_HEADS * TOP_N

    def blk_copy(bb, k, slot):
        j = jnp.minimum(idx_ref[bb * n_slot + k], new_blk - 1)
        src = pt_ref[bb * n_pages + j // halves] * halves + j % halves
        return pltpu.make_async_copy(cache_ref.at[src], buf_ref.at[slot, k], sem_ref.at[slot])

    def start_all(bb, slot):
        for k in range(n_slot):
            blk_copy(bb, k, slot).start()

    @pl.when(b == 0)
    def _():
        start_all(0, 0)

    @pl.when(b + 1 < n_b)
    def _():
        start_all(b + 1, (b + 1) % 2)

    slot = b % 2
    for k in range(n_slot):
        blk_copy(b, k, slot).wait()

    row = lax.broadcasted_iota(jnp.int32, (N_HEADS, LANES), 0)
    lane = lax.broadcasted_iota(jnp.int32, (N_HEADS, LANES), 1)
    grp0 = _mod(row, KV_HEADS) == 0
    q = q_ref[0].astype(F32)
    qe = jnp.zeros((N_HEADS, LANES), F32)
    for v in range(HPG):
        qv = jnp.broadcast_to(q[:, v * LANES:(v + 1) * LANES], (N_HEADS, LANES))
        qe = jnp.where((_div(row, KV_HEADS) == v) & (_div(lane, HEAD_DIM) == _mod(row, KV_HEADS)), qv, qe)
    qb = qe.astype(BF16)

    def new_row(ref):
        r = ref[0].astype(BF16).astype(F32)
        return r[:, :KV_LANES], r[:, KV_LANES:]

    n_keys = TOP_N * SEL_BLK
    klane = lax.broadcasted_iota(jnp.int32, (N_HEADS, n_keys), 1)
    s_g = []
    for g in range(KV_HEADS):
        kg = buf_ref[slot, g * TOP_N:(g + 1) * TOP_N, :, :KV_LANES].reshape(n_keys, KV_LANES).astype(BF16)
        s = _dot_nt(qb, kg)
        for k in range(TOP_N):
            pen = jnp.where(idx_ref[b * n_slot + g * TOP_N + k] >= new_blk, NEG, 0.0)
            s = jnp.where(_div(klane, SEL_BLK) == k, s + pen, s)
        s_g.append(s)
    grp0_k = _mod(lax.broadcasted_iota(jnp.int32, (N_HEADS, n_keys), 0), KV_HEADS) == 0
    s = jnp.where(grp0_k, s_g[0], s_g[1])
    k_new, v_new = new_row(ns_ref)
    s_new = jnp.sum(qe * k_new, axis=-1, keepdims=True)
    m = jnp.maximum(jnp.max(s, axis=-1, keepdims=True), s_new)
    p = jnp.exp(s - m)
    p_new = jnp.exp(s_new - m)
    den = jnp.sum(p, axis=-1, keepdims=True) + p_new
    pb = p.astype(BF16)
    pv = []
    for g in range(KV_HEADS):
        vg = buf_ref[slot, g * TOP_N:(g + 1) * TOP_N, :, KV_LANES:].reshape(n_keys, KV_LANES).astype(BF16)
        pv.append(_dot(pb, vg))
    o_sel = (jnp.where(grp0, pv[0], pv[1]) + p_new.astype(BF16).astype(F32) * v_new) / den

    w_len = win_ref.shape[1]
    wout_ref[0, 0:w_len - 1, :] = win_ref[0, 1:w_len, :]
    wout_ref[0, w_len - 1:w_len, :] = nw_ref[0]
    kw = wout_ref[0, :, :KV_LANES].astype(BF16)
    vw = wout_ref[0, :, KV_LANES:].astype(BF16)
    sw = _dot_nt(qb, kw)
    ew = jnp.exp(sw - jnp.max(sw, axis=-1, keepdims=True))
    o_win = _dot(ew.astype(BF16), vw) / jnp.sum(ew, axis=-1, keepdims=True)

    gn = jnp.broadcast_to(gn_ref[0], (N_HEADS, LANES))
    head = _mod(row, KV_HEADS) * HPG + _div(row, KV_HEADS)

    def gate(branch):
        return jnp.sum(jnp.where(lane == head * N_BRANCH + branch, gn, 0.0), axis=-1, keepdims=True)

    out = gate(0) * ocmp_ref[0] + gate(1) * o_sel + gate(2) * o_win
    o_ref[0] = jnp.concatenate(
        [jnp.sum(jnp.where((_div(row, KV_HEADS) == v) & (_div(lane, HEAD_DIM) == _mod(row, KV_HEADS)), out, 0.0),
                 axis=0, keepdims=True)
         for v in range(HPG)], axis=1).astype(BF16)


def _sample_attn(page_table, idx, cache, q, gn, new_sel, new_win, win_buf, ocmp, *, t_pos):
    bsz, n_pages = page_table.shape
    n_phys, page, _ = cache.shape
    halves = page // SEL_BLK
    w_len = win_buf.shape[1]
    one = lambda n: pl.BlockSpec((1, 1, n), lambda i, *_: (i, 0, 0))
    gs = pltpu.PrefetchScalarGridSpec(
        num_scalar_prefetch=2,
        grid=(bsz,),
        in_specs=[pl.BlockSpec(memory_space=pl.ANY), one(D_ATTN), one(LANES), one(KV_ROW), one(KV_ROW),
                  pl.BlockSpec((1, w_len, KV_ROW), lambda i, *_: (i, 0, 0)),
                  pl.BlockSpec((1, N_HEADS, LANES), lambda i, *_: (i, 0, 0))],
        out_specs=[one(D_ATTN), pl.BlockSpec((1, w_len, KV_ROW), lambda i, *_: (i, 0, 0))],
        scratch_shapes=[pltpu.VMEM((2, KV_HEADS * TOP_N, SEL_BLK, KV_ROW), F32), pltpu.SemaphoreType.DMA((2,))],
    )
    return pl.pallas_call(
        functools.partial(_sample_attn_body, n_pages=n_pages, page=page, t_pos=t_pos),
        grid_spec=gs,
        out_shape=[jax.ShapeDtypeStruct((bsz, 1, D_ATTN), BF16), jax.ShapeDtypeStruct(win_buf.shape, F32)],
        compiler_params=_cparams(1),
        name="sample_attn",
    )(page_table.reshape(-1), idx.reshape(-1), cache.reshape(n_phys * halves, SEL_BLK, KV_ROW), q, gn, new_sel,
      new_win, win_buf, ocmp)


def _rope_tables(pos):
    half = HEAD_DIM // 2
    freqs = jnp.power(ROPE_THETA, -jnp.arange(half, dtype=F32) * (2.0 / HEAD_DIM))
    ang = pos.astype(F32)[:, None] * freqs[None, :]
    cos, sin = jnp.cos(ang), jnp.sin(ang)
    reps = LANES // HEAD_DIM
    return jnp.tile(jnp.concatenate([cos, cos], axis=1), (1, reps)), jnp.tile(jnp.concatenate([-sin, sin], axis=1),
                                                                             (1, reps))


def _compress_weights(w1k, w2k, pek, w1v, w2v, pev):
    n_j = 2 * KV_HEADS
    wbd = jnp.zeros((CMP_STRIDE, n_j, HEAD_DIM, CMP_RATIO, n_j, CMP_HID), F32)
    for j in range(n_j):
        w1 = w1k if j < KV_HEADS else w1v
        wbd = wbd.at[:, j, :, :, j, :].set(w1.transpose(1, 2, 0, 3))
    wbd = wbd.reshape(CMP_STRIDE, KV_ROW, CMP_RATIO * n_j * CMP_HID).astype(BF16)

    def pe_rows(r):
        k = pek[r * CMP_STRIDE:(r + 1) * CMP_STRIDE]
        v = pev[r * CMP_STRIDE:(r + 1) * CMP_STRIDE]
        return jnp.concatenate([k] * KV_HEADS + [v] * KV_HEADS, axis=1)

    def pair(w2):
        z = jnp.zeros_like(w2)
        return jnp.concatenate([jnp.concatenate([w2, z], axis=1), jnp.concatenate([z, w2], axis=1)],
                               axis=0).astype(BF16)

    return pe_rows(0), pe_rows(1), wbd, pair(w2k), pair(w2v)


def _cmp_to_sel(n_cmp_pad, n_sel_pad, n_cmp, n_sel):
    i = np.arange(n_cmp_pad)[:, None]
    j = np.arange(n_sel_pad)[None, :]
    start = i * CMP_STRIDE
    hit = (start < j * SEL_BLK + SEL_BLK) & (start + CMP_BLK > j * SEL_BLK) & (i < n_cmp) & (j < n_sel)
    return jnp.asarray(hit, BF16)


def kernel(x_prompt, x_sample, cache_cmp_kv, cache_sel_kv, state_win_kv, state_conv, page_table, norm_ffn1, ffn1_w_gate, ffn1_w_up, ffn1_w_down, norm_mix, w_mix_in, conv_w, w_cmpk1, w_cmpk2, pe_cmpk, w_cmpv1, w_cmpv2, pe_cmpv, w_conv_out, w_attn_out, w_mix_out, norm_ffn2, ffn2_w_gate, ffn2_w_up, ffn2_w_down, norm_final):
    depth = norm_ffn1.shape[0]
    assert depth == 1
    bsz, t_len, _ = x_prompt.shape
    dbsz, dseq, _ = x_sample.shape
    n_pages = page_table.shape[1]
    page = cache_cmp_kv.shape[2]
    past = n_pages * page
    w_len = state_win_kv.shape[2]
    assert dseq == 1 and w_len == WINDOW and past >= WINDOW and past % SEL_BLK == 0 and page % SEL_BLK == 0
    assert t_len >= WINDOW and t_len // CMP_STRIDE == LANES

    offs = np.cumsum(IN_SPLITS)[:-1].tolist()
    w_ub, w_uc, w_ux, w_q, w_kvc, w_kvs, w_kvw, w_gn, w_gb = jnp.split(w_mix_in[0], offs, axis=1)
    order = np.asarray(HEAD_ORDER)
    w_q = w_q.reshape(D_MODEL, N_HEADS, HEAD_DIM)[:, order].reshape(D_MODEL, D_ATTN)
    w_gn = jnp.pad(w_gn, ((0, 0), (0, LANES - w_gn.shape[1])))
    w_a = jnp.concatenate([w_uc, w_ux, w_q, w_kvc, w_kvs, w_kvw, w_gn], axis=1).astype(BF16)
    w_b = jnp.concatenate([w_ub, w_gb], axis=1).astype(BF16)
    w_ao = w_attn_out[0].reshape(N_HEADS, HEAD_DIM, D_MODEL)[order].reshape(D_ATTN, D_MODEL).astype(BF16)
    w_co = w_conv_out[0].astype(BF16)
    w_mo = w_mix_out[0].astype(BF16)
    cw = _compress_weights(w_cmpk1[0], w_cmpk2[0], pe_cmpk[0], w_cmpv1[0], w_cmpv2[0], pe_cmpv[0])
    ffn1 = (norm_ffn1, ffn1_w_gate[0].astype(BF16), ffn1_w_up[0].astype(BF16), ffn1_w_down[0].astype(BF16))
    ffn2 = (norm_ffn2, ffn2_w_gate[0].astype(BF16), ffn2_w_up[0].astype(BF16), ffn2_w_down[0].astype(BF16))
    g_final = norm_final.reshape(1, D_MODEL)

    tm = 512
    xp = x_prompt.reshape(bsz * t_len, D_MODEL)
    xp = _ffn(xp, *ffn1, g_final, final_norm=False, tm=tm)
    cos, sin = _rope_tables(jnp.arange(t_len, dtype=jnp.int32))
    uin, q, rc, rs, rw, ks, kw, gn = _mixin(xp, norm_mix, w_a, cos, sin, tm=tm, pos_blocks=t_len // tm)
    b3 = lambda a: a.reshape(bsz, t_len, a.shape[-1])
    kc, vc = _compress_prompt(b3(rc), cw)
    n_chunks = t_len // CMP_STRIDE
    n_sel = t_len // SEL_BLK
    e_mat = jnp.asarray(np.arange(t_len)[:, None] // SEL_BLK == np.arange(LANES)[None, :], BF16)
    c2s = _cmp_to_sel(n_chunks, LANES, n_chunks - CMP_RATIO + 1, n_sel)
    o = _nsa_prompt(b3(q), b3(gn), kc, vc, b3(ks), b3(kw), e_mat, c2s, tq=128, kt=256)
    o = o.reshape(bsz * t_len, D_ATTN)
    xp = _merge(xp, norm_mix, uin, uin, uin, o, w_b, conv_w[0], w_co, w_ao, w_mo, tm=tm, tiles_per_seq=t_len // tm)
    y_prompt = _ffn(xp, *ffn2, g_final, final_norm=True, tm=tm).reshape(bsz, t_len, D_MODEL)

    kv6 = lambda a, n, t: a.reshape(1, n, t, 2, KV_HEADS, HEAD_DIM)
    cmp_p = kv6(rc, bsz, t_len)
    sel_p = kv6(rs, bsz, t_len)
    win_p = kv6(rw, bsz, t_len)[:, :, t_len - WINDOW:]
    conv_p = uin.reshape(1, bsz, t_len, D_CONV)[:, :, t_len - (CONV_W - 1):]

    xs = x_sample.reshape(dbsz, D_MODEL)
    xs = _ffn(xs, *ffn1, g_final, final_norm=False, tm=dbsz)
    cos_s, sin_s = _rope_tables(jnp.full((dbsz,), past, jnp.int32))
    uin_s, q_s, rc_s, rs_s, rw_s, _, _, gn_s = _mixin(xs, norm_mix, w_a, cos_s, sin_s, tm=dbsz, pos_blocks=1)
    one = lambda a: a.reshape(dbsz, 1, a.shape[-1])
    n_chunks_s = past // CMP_STRIDE
    n_sel_s = past // SEL_BLK + 1
    n_sel_pad = -(-n_sel_s // LANES) * LANES
    c2s_s = _cmp_to_sel(n_chunks_s, n_sel_pad, (past + 1) // CMP_STRIDE - CMP_RATIO + 1, n_sel_s)
    cache_c = cache_cmp_kv[0].reshape(-1, page, KV_ROW)
    cache_s = cache_sel_kv[0].reshape(-1, page, KV_ROW)
    ocmp, idx = _sample_cmp(page_table, cache_c, one(q_s), cw, c2s_s, t_pos=past)
    idx = idx[:, :KV_HEADS, :TOP_N]
    o_s, win_s = _sample_attn(page_table, idx, cache_s, one(q_s), one(gn_s), one(rs_s), one(rw_s),
                              state_win_kv[0].reshape(dbsz, w_len, KV_ROW), ocmp, t_pos=past)
    xs = _merge(xs, norm_mix, uin_s, state_conv[0, :, 1], state_conv[0, :, 0], o_s.reshape(dbsz, D_ATTN), w_b,
                conv_w[0], w_co, w_ao, w_mo, tm=dbsz, tiles_per_seq=None)
    y_sample = _ffn(xs, *ffn2, g_final, final_norm=True, tm=dbsz).reshape(dbsz, 1, D_MODEL)

    cmp_s = kv6(rc_s, dbsz, 1)
    sel_s = kv6(rs_s, dbsz, 1)
    win_s = kv6(win_s, dbsz, w_len)
    conv_s = jnp.concatenate([state_conv[0, :, 1:], uin_s[:, None, :]], axis=1)[None]
    return (y_prompt, y_sample, cmp_p, sel_p, win_p, conv_p, cmp_s, sel_s, win_s, conv_s)
```

```python
import functools

import numpy as np
import jax
import jax.numpy as jnp
from jax import lax
from jax.experimental import pallas as pl
from jax.experimental.pallas import tpu as pltpu

F32 = jnp.float32
BF16 = jnp.bfloat16

D_MODEL = 1024
D_CONV = 512
CONV_W = 3
N_HEADS = 8
KV_HEADS = 2
HPG = N_HEADS // KV_HEADS
HEAD_DIM = 64
D_ATTN = N_HEADS * HEAD_DIM
KV_ROW = 2 * KV_HEADS * HEAD_DIM
CMP_STRIDE = 16
CMP_RATIO = 2
CMP_BLK = CMP_STRIDE * CMP_RATIO
CMP_HID = 128
SEL_BLK = 64
TOP_N = 16
WINDOW = 512
N_BRANCH = 3
D_FF = 2816
ROPE_THETA = 10000.0
EPS = 1e-6
NEG = -1e30
BIG = 1e9
GONE = -3e38
IN_SPLITS = (D_CONV, D_CONV, D_CONV, D_ATTN, KV_ROW, KV_ROW, KV_ROW, N_HEADS * N_BRANCH, 2 * D_MODEL)

LANES = 128
KV_LANES = KV_HEADS * HEAD_DIM
assert KV_LANES == LANES and KV_HEADS == 2 and 2 * HEAD_DIM == LANES
HEAD_ORDER = tuple(half * HPG + v for v in range(HPG) for half in range(KV_HEADS))
VMEM_LIMIT = 56 * 1024 * 1024
ROW_TILE = 512
Q_TILE = 128
KEY_TILE = 256


def _cparams(n_grid):
    return pltpu.CompilerParams(dimension_semantics=("arbitrary",) * n_grid, vmem_limit_bytes=VMEM_LIMIT)


def _const_spec(shape):
    nd = len(shape)
    return pl.BlockSpec(shape, lambda *_: (0,) * nd, pipeline_mode=pl.Buffered(1))


def _rms(x, g):
    return x * lax.rsqrt(jnp.mean(x * x, axis=-1, keepdims=True) + EPS) * g


def _silu(x):
    return x * jax.nn.sigmoid(x)


def _div(x, n):
    return lax.shift_right_logical(x, int(np.log2(n)))


def _mod(x, n):
    return x & (n - 1)


def _dot(a, b):
    return jnp.dot(a, b, preferred_element_type=F32)


def _dot_nt(a, b):
    return lax.dot_general(a, b, (((1,), (1,)), ((), ())), preferred_element_type=F32)


def _ffn_body(x_ref, g_ref, wg_ref, wu_ref, wd_ref, gf_ref, o_ref, *, final_norm):
    x = x_ref[...]
    h = _rms(x, g_ref[...]).astype(BF16)
    a = (_silu(_dot(h, wg_ref[...])) * _dot(h, wu_ref[...])).astype(BF16)
    y = x + 0.5 * _dot(a, wd_ref[...])
    if final_norm:
        y = _rms(y, gf_ref[...])
    o_ref[...] = y


def _ffn(x, g, wg, wu, wd, gf, *, final_norm):
    m = x.shape[0]
    tm = min(ROW_TILE, m)
    row = pl.BlockSpec((tm, D_MODEL), lambda i: (i, 0))
    return pl.pallas_call(
        functools.partial(_ffn_body, final_norm=final_norm),
        grid=(m // tm,),
        in_specs=[row, _const_spec((1, D_MODEL)), _const_spec(wg.shape), _const_spec(wu.shape),
                  _const_spec(wd.shape), _const_spec((1, D_MODEL))],
        out_specs=row,
        out_shape=jax.ShapeDtypeStruct((m, D_MODEL), F32),
        compiler_params=_cparams(1),
        name="ffn",
    )(x, g, wg, wu, wd, gf)


ROW_COLS = {"uc": (0, 512), "ux": (512, 1024), "q": (1024, 1536), "gn": (1536, 1664), "kvs": (1664, 1920),
            "kvw": (1920, 2176)}


def _rope_tile(z, cos, sin):
    first = _mod(lax.broadcasted_iota(jnp.int32, z.shape, 1), HEAD_DIM) < (HEAD_DIM // 2)
    rot = jnp.where(first, pltpu.roll(z, LANES - HEAD_DIM // 2, axis=1), pltpu.roll(z, HEAD_DIM // 2, axis=1))
    return z * cos + rot * sin


def _rope_rows_t(zt, cos_t, sin_t):
    half = HEAD_DIM // 2
    out = []
    for hh in range(KV_LANES // HEAD_DIM):
        x1 = zt[hh * HEAD_DIM:hh * HEAD_DIM + half]
        x2 = zt[hh * HEAD_DIM + half:(hh + 1) * HEAD_DIM]
        out += [x1 * cos_t - x2 * sin_t, x1 * sin_t + x2 * cos_t]
    return jnp.concatenate(out, axis=0)


def _mixin_body(x_ref, g_ref, wr_ref, wt_ref, cos_ref, sin_ref, cost_ref, sint_ref,
                uin_ref, q_ref, gn_ref, rct_ref, rst_ref, rwt_ref, kst_ref, kwt_ref, *row_refs):
    h = _rms(x_ref[0], g_ref[...]).astype(BF16)
    cos = cos_ref[...]
    sin = sin_ref[...]

    def proj(name):
        lo, hi = ROW_COLS[name]
        return _dot(h, wr_ref[:, lo:hi])

    uin_ref[0] = proj("uc") * proj("ux")
    zq = proj("q")
    scale = HEAD_DIM ** -0.5
    q_ref[0] = jnp.concatenate(
        [_rope_tile(zq[:, v * LANES:(v + 1) * LANES], cos, sin) * scale for v in range(D_ATTN // LANES)],
        axis=1).astype(BF16)
    gn_ref[0] = jax.nn.sigmoid(proj("gn"))
    cos_t = cost_ref[...]
    sin_t = sint_ref[...]
    for i, (f_ref, b_ref) in enumerate(((rct_ref, None), (rst_ref, kst_ref), (rwt_ref, kwt_ref))):
        zt = _dot_nt(wt_ref[i * KV_ROW:(i + 1) * KV_ROW, :], h)
        rows_t = jnp.concatenate([_rope_rows_t(zt[:KV_LANES], cos_t, sin_t), zt[KV_LANES:]], axis=0)
        f_ref[0] = rows_t
        if b_ref is not None:
            b_ref[0] = rows_t.astype(BF16)
    for name, r_ref in zip(("kvs", "kvw"), row_refs):
        z = proj(name)
        r_ref[0] = jnp.concatenate([_rope_tile(z[:, :KV_LANES], cos, sin), z[:, KV_LANES:]], axis=1)


def _mixin(x, g, wr, wt, tabs, *, rows_out):
    b, t, _ = x.shape
    tm = min(ROW_TILE, t)
    row = lambda n: pl.BlockSpec((1, tm, n), lambda i, j: (i, j, 0))
    col = pl.BlockSpec((1, KV_ROW, tm), lambda i, j: (i, 0, j))
    tab = pl.BlockSpec((tm, LANES), lambda i, j: (j, 0))
    tab_t = pl.BlockSpec((HEAD_DIM // 2, tm), lambda i, j: (0, j))
    rsd = lambda n, dt: jax.ShapeDtypeStruct((b, t, n), dt)
    csd = lambda dt: jax.ShapeDtypeStruct((b, KV_ROW, t), dt)
    extra = 2 if rows_out else 0
    return pl.pallas_call(
        _mixin_body,
        grid=(b, t // tm),
        in_specs=[row(D_MODEL), _const_spec((1, D_MODEL)), _const_spec(wr.shape), _const_spec(wt.shape),
                  tab, tab, tab_t, tab_t],
        out_specs=[row(D_CONV), row(D_ATTN), row(LANES), col, col, col, col, col] + [row(KV_ROW)] * extra,
        out_shape=[rsd(D_CONV, F32), rsd(D_ATTN, BF16), rsd(LANES, F32), csd(F32), csd(F32), csd(F32), csd(BF16),
                   csd(BF16)] + [rsd(KV_ROW, F32)] * extra,
        compiler_params=_cparams(2),
        name="mix_in",
    )(x, g, wr, wt, *tabs)


def _compress(rows_ref, n_chunks, pe0_ref, pe1_ref, wbd_ref, w2k_ref, w2v_ref):
    half = KV_ROW * CMP_RATIO
    acc0 = jnp.zeros((n_chunks, half), F32)
    acc1 = jnp.zeros((n_chunks, half), F32)
    for s in range(CMP_STRIDE):
        a = jnp.concatenate([rows_ref[kv, pl.ds(s, n_chunks, stride=CMP_STRIDE), :] for kv in range(2)], axis=1)
        acc0 = acc0 + _dot((a + pe0_ref[s:s + 1, :]).astype(BF16), wbd_ref[s, :, :half])
        acc1 = acc1 + _dot((a + pe1_ref[s:s + 1, :]).astype(BF16), wbd_ref[s, :, half:])
    hid = acc0 + pltpu.roll(acc1, n_chunks - 1, axis=0)
    act = _silu(hid).astype(BF16)
    keep = lax.broadcasted_iota(jnp.int32, (n_chunks, LANES), 0) < n_chunks - 1
    kc = jnp.where(keep, _dot(act[:, :half // 2], w2k_ref[...]), 0.0)
    vc = jnp.where(keep, _dot(act[:, half // 2:], w2v_ref[...]), 0.0)
    return kc, vc


def _compress_prompt_body(xt_ref, pe0_ref, pe1_ref, wbd_ref, w2k_ref, w2v_ref, kct_ref, vc_ref, rows_ref,
                          *, n_chunks, t_len):
    for kv in range(2):
        for c in range(t_len // LANES):
            rows_ref[kv, c * LANES:(c + 1) * LANES, :] = (
                xt_ref[0, kv * KV_LANES:(kv + 1) * KV_LANES, c * LANES:(c + 1) * LANES].T)
    kc, vc = _compress(rows_ref, n_chunks, pe0_ref, pe1_ref, wbd_ref, w2k_ref, w2v_ref)
    kct_ref[0] = kc.T.astype(BF16)
    vc_ref[0] = vc.astype(BF16)


def _compress_prompt(rct, cw):
    b, _, t = rct.shape
    n_chunks = t // CMP_STRIDE
    out = pl.BlockSpec((1, n_chunks, LANES), lambda i: (i, 0, 0))
    return pl.pallas_call(
        functools.partial(_compress_prompt_body, n_chunks=n_chunks, t_len=t),
        grid=(b,),
        in_specs=[pl.BlockSpec((1, KV_ROW, t), lambda i: (i, 0, 0))] + [_const_spec(w.shape) for w in cw],
        out_specs=[out, out],
        out_shape=[jax.ShapeDtypeStruct((b, n_chunks, LANES), BF16)] * 2,
        scratch_shapes=[pltpu.VMEM((2, t, LANES), F32)],
        compiler_params=_cparams(1),
        name="compress_prompt",
    )(rct, *cw)


def _softmax_rows(s):
    e = jnp.exp(s - jnp.max(s, axis=-1, keepdims=True))
    return e / jnp.sum(e, axis=-1, keepdims=True)


def _split_hi_lo(x):
    hi = x.astype(BF16)
    return hi, (x - hi.astype(F32)).astype(BF16)


def _nsa_prompt_body(q_ref, gn_ref, kct_ref, vc_ref, ks_ref, kw_ref, e_ref, c2st_ref, o_ref,
                     qaug_ref, p_ref, m_ref, l_ref, acc_ref, alpha_ref, out_ref, *, tq, kt, t_len):
    n_slots = N_HEADS
    rows = n_slots * tq
    n_sel = t_len // SEL_BLK
    start = pl.program_id(1) * tq
    lane = lax.broadcasted_iota(jnp.int32, (tq, LANES), 1)
    t_pos = start + lax.broadcasted_iota(jnp.int32, (tq, LANES), 0)
    low = lane < HEAD_DIM
    gn = gn_ref[0]
    slot_rows = [slice(slot * tq, (slot + 1) * tq) for slot in range(n_slots)]

    def gate(slot, branch):
        v, half = divmod(slot, KV_HEADS)
        c = (half * HPG + v) * N_BRANCH + branch
        return gn[:, c:c + 1]

    q = q_ref[0].astype(F32)
    for slot, rs in enumerate(slot_rows):
        v, half = divmod(slot, KV_HEADS)
        qv = q[:, v * LANES:(v + 1) * LANES]
        qaug_ref[rs, :LANES] = (jnp.where(low, qv, 0.0) if half == 0 else jnp.where(low, 0.0, qv)).astype(BF16)

    cmp_ok = lane * CMP_STRIDE + (CMP_BLK - 1) <= t_pos
    s_c = _dot(qaug_ref[:, :LANES], kct_ref[0])
    p_c = jnp.concatenate(
        [jnp.where(cmp_ok, _softmax_rows(jnp.where(cmp_ok, s_c[rs], NEG)), 0.0) for rs in slot_rows], axis=0)
    o_c = _dot(p_c.astype(BF16), vc_ref[0])
    for slot, rs in enumerate(slot_rows):
        out_ref[rs, :] = gate(slot, 0) * o_c[rs]

    blk = lax.broadcasted_iota(jnp.int32, (n_sel, tq), 0)
    cur = _div(start + lax.broadcasted_iota(jnp.int32, (n_sel, tq), 1), SEL_BLK)
    valid = blk <= cur
    forced = valid & ((blk == 0) | (blk == cur) | (blk == cur - 1))
    for half in range(KV_HEADS):
        psum = p_c[slot_rows[half]]
        for v in range(1, HPG):
            psum = psum + p_c[slot_rows[v * KV_HEADS + half]]
        hi, lo = _split_hi_lo(psum)
        imp = (_dot_nt(c2st_ref[...], hi) + _dot_nt(c2st_ref[...], lo))[:n_sel]
        score = jnp.where(forced, BIG, jnp.where(valid, imp, -BIG))
        rank = jnp.zeros((n_sel, tq), F32)
        for i in range(n_sel):
            r = score[i:i + 1, :]
            tie = jnp.where(blk > i, 1.0, 0.0)
            rank = rank + jnp.where(r > score, 1.0, jnp.where(r == score, tie, 0.0))
        pen_t = jnp.concatenate([jnp.where(rank >= TOP_N, NEG, 0.0), jnp.zeros((LANES - n_sel, tq), F32)], axis=0)
        pen = pen_t.T.astype(BF16)
        for v in range(HPG):
            qaug_ref[slot_rows[v * KV_HEADS + half], LANES:] = pen

    d = lax.broadcasted_iota(jnp.int32, (tq, kt), 1) - lax.broadcasted_iota(jnp.int32, (tq, kt), 0)
    ones = jnp.ones((KV_LANES, kt), BF16)

    def reset():
        m_ref[...] = jnp.full((rows, LANES), NEG, F32)
        l_ref[...] = jnp.zeros((rows, LANES), F32)
        acc_ref[...] = jnp.zeros((rows, LANES), F32)

    def tile_step(kv_ref, koff, select, masked, window):
        kv = kv_ref[0, :, pl.ds(koff, kt)]
        if select:
            s_all = _dot(qaug_ref[...], jnp.concatenate([kv[:KV_LANES], e_ref[:, pl.ds(koff, kt)]], axis=0))
        else:
            s_all = _dot(qaug_ref[:, :LANES], kv[:KV_LANES])
        if masked:
            rel = d + (koff - start)
            ok = (rel <= 0) & (rel > -WINDOW) if window else rel <= 0
        for rs in slot_rows:
            s = jnp.where(ok, s_all[rs], NEG) if masked else s_all[rs]
            m_prev = m_ref[rs, :]
            m_new = jnp.maximum(m_prev, jnp.max(s, axis=-1, keepdims=True))
            alpha_ref[rs, :] = jnp.exp(m_prev - m_new)
            p_ref[rs, :] = jnp.exp(s - jnp.concatenate([m_new] * (kt // LANES), axis=1)).astype(BF16)
            m_ref[rs, :] = m_new
        pv = _dot_nt(p_ref[...], jnp.concatenate([kv[KV_LANES:], ones], axis=0))
        acc_ref[...] = alpha_ref[...] * acc_ref[...] + pv[:, :LANES]
        l_ref[...] = alpha_ref[...] * l_ref[...] + pv[:, LANES:]

    def finish(branch):
        for slot, rs in enumerate(slot_rows):
            out_ref[rs, :] = out_ref[rs, :] + gate(slot, branch) * (acc_ref[rs, :] / l_ref[rs, :])

    reset()
    n_full = start // kt

    def sel_body(j, carry):
        tile_step(ks_ref, pl.multiple_of(j * kt, kt), True, False, False)
        return carry

    lax.fori_loop(0, n_full, sel_body, 0)
    tile_step(ks_ref, pl.multiple_of(n_full * kt, kt), True, True, False)
    finish(1)

    reset()
    first_tile = lax.shift_right_arithmetic(start - WINDOW, int(np.log2(kt)))
    for jw in range(3):
        ti = first_tile + jw

        @pl.when(ti >= 0)
        def _():
            tile_step(kw_ref, pl.multiple_of(ti * kt, kt), False, jw != 1, True)
    finish(2)

    o_ref[0] = jnp.concatenate(
        [jnp.where(low, out_ref[slot_rows[2 * v], :], out_ref[slot_rows[2 * v + 1], :]) for v in range(HPG)],
        axis=1).astype(BF16)


def _nsa_prompt(q, gn, kct, vc, kst, kwt, e_t, c2s_t):
    b, t, _ = q.shape
    tq, kt = Q_TILE, KEY_TILE
    assert kt == 2 * tq and WINDOW % kt == 0 and t % kt == 0
    n_cmp = vc.shape[1]
    rows = N_HEADS * tq
    tile = lambda n: pl.BlockSpec((1, tq, n), lambda i, j: (i, j, 0))
    per_b = lambda r, n: pl.BlockSpec((1, r, n), lambda i, j: (i, 0, 0))
    rows_f32 = pltpu.VMEM((rows, LANES), F32)
    return pl.pallas_call(
        functools.partial(_nsa_prompt_body, tq=tq, kt=kt, t_len=t),
        grid=(b, t // tq),
        in_specs=[tile(D_ATTN), tile(LANES), per_b(LANES, n_cmp), per_b(n_cmp, LANES), per_b(KV_ROW, t),
                  per_b(KV_ROW, t), _const_spec(e_t.shape), _const_spec(c2s_t.shape)],
        out_specs=tile(D_ATTN),
        out_shape=jax.ShapeDtypeStruct((b, t, D_ATTN), BF16),
        scratch_shapes=[pltpu.VMEM((rows, 2 * LANES), BF16), pltpu.VMEM((rows, kt), BF16), rows_f32, rows_f32,
                        rows_f32, rows_f32, rows_f32],
        compiler_params=_cparams(2),
        name="nsa_prompt",
    )(q, gn, kct, vc, kst, kwt, e_t, c2s_t)


def _merge_body(x_ref, g_ref, u_ref, um1_ref, um2_ref, o_ref, wb_ref, cw_ref, wco_ref, wao_ref, wmo_ref,
                y_ref, *, tm, tiles_per_seq):
    x = x_ref[...]
    h = _rms(x, g_ref[...]).astype(BF16)
    ub = _dot(h, wb_ref[:, :D_CONV])
    u = u_ref[...]
    if tiles_per_seq is None:
        um1 = um1_ref[...]
        um2 = um2_ref[...]
    else:
        row = lax.broadcasted_iota(jnp.int32, (tm, D_CONV), 0)
        keep = jnp.where(pl.program_id(0) % tiles_per_seq == 0, 0.0, 1.0)
        prev = um1_ref[...] * keep
        um1 = jnp.where(row == 0, prev[7:8, :], pltpu.roll(u, 1, axis=0))
        um2 = jnp.where(row == 0, prev[6:7, :], jnp.where(row == 1, prev[7:8, :], pltpu.roll(u, 2, axis=0)))
    conv = cw_ref[0:1, :] * um2 + cw_ref[1:2, :] * um1 + cw_ref[2:3, :] * u
    y_c = _dot((ub * conv).astype(BF16), wco_ref[...])
    y_a = _dot(o_ref[...], wao_ref[...])
    g_c = jax.nn.sigmoid(_dot(h, wb_ref[:, D_CONV:D_CONV + D_MODEL]))
    g_a = jax.nn.sigmoid(_dot(h, wb_ref[:, D_CONV + D_MODEL:]))
    y_ref[...] = x + _dot((g_c * y_c + g_a * y_a).astype(BF16), wmo_ref[...])


def _merge(x, g, uin, um1, um2, o, wb, cw, wco, wao, wmo, *, seq_len):
    m = x.shape[0]
    tm = min(ROW_TILE, m)
    row = lambda n: pl.BlockSpec((tm, n), lambda i: (i, 0))
    if seq_len is None:
        prev_specs = [row(D_CONV), row(D_CONV)]
        tiles_per_seq = None
    else:
        halo = pl.BlockSpec((8, D_CONV), lambda i: (jnp.maximum(i * (tm // 8) - 1, 0), 0))
        prev_specs = [halo, halo]
        tiles_per_seq = seq_len // tm
    return pl.pallas_call(
        functools.partial(_merge_body, tm=tm, tiles_per_seq=tiles_per_seq),
        grid=(m // tm,),
        in_specs=[row(D_MODEL), _const_spec((1, D_MODEL)), row(D_CONV)] + prev_specs +
                 [row(D_ATTN), _const_spec(wb.shape), _const_spec(cw.shape), _const_spec(wco.shape),
                  _const_spec(wao.shape), _const_spec(wmo.shape)],
        out_specs=row(D_MODEL),
        out_shape=jax.ShapeDtypeStruct((m, D_MODEL), F32),
        compiler_params=_cparams(1),
        name="merge",
    )(x, g, uin, um1, um2, o, wb, cw, wco, wao, wmo)


def _expand_q(q_ref):
    row = lax.broadcasted_iota(jnp.int32, (N_HEADS, LANES), 0)
    lane = lax.broadcasted_iota(jnp.int32, (N_HEADS, LANES), 1)
    q = q_ref[0].astype(F32)
    qe = jnp.zeros((N_HEADS, LANES), F32)
    for v in range(HPG):
        qv = jnp.broadcast_to(q[:, v * LANES:(v + 1) * LANES], (N_HEADS, LANES))
        qe = jnp.where((_div(row, KV_HEADS) == v) & (_div(lane, HEAD_DIM) == _mod(row, KV_HEADS)), qv, qe)
    return qe


def _sample_cmp_body(pt_ref, cache_ref, q_ref, pe0_ref, pe1_ref, wbd_ref, w2k_ref, w2v_ref, c2s_ref,
                     ocmp_ref, idx_ref, buf_ref, rows_ref, sem_ref, *, n_pages, page, t_pos):
    b = pl.program_id(0)
    n_b = pl.num_programs(0)
    n_chunks = n_pages * page // CMP_STRIDE
    n_sel = t_pos // SEL_BLK + 1

    def page_copy(bb, p, slot):
        return pltpu.make_async_copy(cache_ref.at[pt_ref[bb * n_pages + p]], buf_ref.at[slot, p], sem_ref.at[slot])

    def start_all(bb, slot):
        for p in range(n_pages):
            page_copy(bb, p, slot).start()

    @pl.when(b == 0)
    def _():
        start_all(0, 0)

    @pl.when(b + 1 < n_b)
    def _():
        start_all(b + 1, (b + 1) % 2)

    slot = b % 2
    for p in range(n_pages):
        page_copy(b, p, slot).wait()

    def to_rows(p, carry):
        for kv in range(2):
            rows_ref[kv, pl.ds(pl.multiple_of(p * page, page), page), :] = buf_ref[slot, p, kv].T
        return carry

    lax.fori_loop(0, n_pages, to_rows, 0)
    kc, vc = _compress(rows_ref, n_chunks, pe0_ref, pe1_ref, wbd_ref, w2k_ref, w2v_ref)

    lane = lax.broadcasted_iota(jnp.int32, (N_HEADS, LANES), 1)
    qe = _expand_q(q_ref)
    cidx = lax.broadcasted_iota(jnp.int32, (N_HEADS, n_chunks), 1)
    cmp_ok = cidx * CMP_STRIDE + (CMP_BLK - 1) <= t_pos
    p = jnp.where(cmp_ok, _softmax_rows(jnp.where(cmp_ok, _dot_nt(qe.astype(BF16), kc.astype(BF16)), NEG)), 0.0)
    ocmp_ref[0] = _dot(p.astype(BF16), vc.astype(BF16))

    hi, lo = _split_hi_lo(p)
    imp_h = _dot(hi, c2s_ref[...]) + _dot(lo, c2s_ref[...])
    n_lanes = c2s_ref.shape[1]
    grow = lax.broadcasted_iota(jnp.int32, (N_HEADS, n_lanes), 0)
    glane = lax.broadcasted_iota(jnp.int32, (N_HEADS, n_lanes), 1)
    imp = jnp.zeros((N_HEADS, n_lanes), F32)
    for g in range(KV_HEADS):
        tot = jnp.sum(jnp.where(_mod(grow, KV_HEADS) == g, imp_h, 0.0), axis=0, keepdims=True)
        imp = jnp.where(grow == g, jnp.broadcast_to(tot, imp.shape), imp)
    cur = t_pos // SEL_BLK
    forced = (glane == 0) | (glane == cur) | (glane == cur - 1)
    score = jnp.where(glane < n_sel, jnp.where(forced, BIG, imp), GONE)
    flane = glane.astype(F32)
    picked = jnp.zeros((N_HEADS, LANES), jnp.int32)
    for k in range(TOP_N):
        best = jnp.max(score, axis=-1, keepdims=True)
        where_best = jnp.min(jnp.where(score == best, flane, float(n_lanes)), axis=-1, keepdims=True)
        picked = jnp.where(lane == k, where_best.astype(jnp.int32), picked)
        score = jnp.where(flane == where_best, GONE, score)
    idx_ref[0] = picked


def _sample_cmp(page_table, cache, q, cw, c2s, *, t_pos):
    bsz, n_pages = page_table.shape
    page = cache.shape[-1]
    assert page == LANES
    gs = pltpu.PrefetchScalarGridSpec(
        num_scalar_prefetch=1,
        grid=(bsz,),
        in_specs=[pl.BlockSpec(memory_space=pl.ANY), pl.BlockSpec((1, 1, D_ATTN), lambda i, pt: (i, 0, 0))] +
                 [pl.BlockSpec(w.shape, (lambda nd: lambda i, pt: (0,) * nd)(w.ndim), pipeline_mode=pl.Buffered(1))
                  for w in (*cw, c2s)],
        out_specs=[pl.BlockSpec((1, N_HEADS, LANES), lambda i, pt: (i, 0, 0))] * 2,
        scratch_shapes=[pltpu.VMEM((2, n_pages, 2, KV_LANES, page), F32),
                        pltpu.VMEM((2, n_pages * page, KV_LANES), F32), pltpu.SemaphoreType.DMA((2,))],
    )
    return pl.pallas_call(
        functools.partial(_sample_cmp_body, n_pages=n_pages, page=page, t_pos=t_pos),
        grid_spec=gs,
        out_shape=[jax.ShapeDtypeStruct((bsz, N_HEADS, LANES), F32),
                   jax.ShapeDtypeStruct((bsz, N_HEADS, LANES), jnp.int32)],
        compiler_params=_cparams(1),
        name="sample_cmp",
    )(page_table.reshape(-1), cache, q, *cw, c2s)


def _sample_attn_body(pt_ref, idx_ref, cache_ref, q_ref, gn_ref, ns_ref, nwt_ref, win_ref, ocmp_ref,
                      o_ref, wout_ref, buf_ref, sem_ref, *, n_pages, page, t_pos):
    b = pl.program_id(0)
    n_b = pl.num_programs(0)
    halves = page // SEL_BLK
    new_blk = t_pos // SEL_BLK
    n_slot = KV_HEADS * TOP_N

    def blk_of(bb, k):
        return idx_ref[bb * n_slot + k]

    def page_copy(bb, k, slot):
        j = jnp.minimum(blk_of(bb, k), new_blk - 1)
        return pltpu.make_async_copy(cache_ref.at[pt_ref[bb * n_pages + j // halves]], buf_ref.at[slot, k],
                                     sem_ref.at[slot])

    def start_all(bb, slot):
        for k in range(n_slot):
            page_copy(bb, k, slot).start()

    @pl.when(b == 0)
    def _():
        start_all(0, 0)

    @pl.when(b + 1 < n_b)
    def _():
        start_all(b + 1, (b + 1) % 2)

    slot = b % 2
    for k in range(n_slot):
        page_copy(b, k, slot).wait()

    row = lax.broadcasted_iota(jnp.int32, (N_HEADS, LANES), 0)
    lane = lax.broadcasted_iota(jnp.int32, (N_HEADS, LANES), 1)
    grp0 = _mod(row, KV_HEADS) == 0
    qe = _expand_q(q_ref)
    qb = qe.astype(BF16)

    n_keys = TOP_N * page
    klane = lax.broadcasted_iota(jnp.int32, (N_HEADS, n_keys), 1)
    s_g = []
    for g in range(KV_HEADS):
        kt_g = jnp.concatenate([buf_ref[slot, g * TOP_N + k, 0] for k in range(TOP_N)], axis=1).astype(BF16)
        pen = jnp.full((N_HEADS, n_keys), NEG, F32)
        for k in range(TOP_N):
            j = blk_of(b, g * TOP_N + k)
            lo = k * page + (j % halves) * SEL_BLK
            seen = jnp.where(j >= new_blk, NEG, 0.0)
            pen = jnp.where((klane >= lo) & (klane < lo + SEL_BLK), seen, pen)
        s_g.append(_dot(qb, kt_g) + pen)
    grp0_k = _mod(lax.broadcasted_iota(jnp.int32, (N_HEADS, n_keys), 0), KV_HEADS) == 0
    s = jnp.where(grp0_k, s_g[0], s_g[1])
    r_new = ns_ref[0].astype(BF16).astype(F32)
    k_new, v_new = r_new[:, :KV_LANES], r_new[:, KV_LANES:]
    s_new = jnp.sum(qe * k_new, axis=-1, keepdims=True)
    m = jnp.maximum(jnp.max(s, axis=-1, keepdims=True), s_new)
    p = jnp.exp(s - m)
    p_new = jnp.exp(s_new - m)
    den = jnp.sum(p, axis=-1, keepdims=True) + p_new
    pb = p.astype(BF16)
    pv = []
    for g in range(KV_HEADS):
        vt_g = jnp.concatenate([buf_ref[slot, g * TOP_N + k, 1] for k in range(TOP_N)], axis=1).astype(BF16)
        pv.append(_dot_nt(pb, vt_g))
    o_sel = (jnp.where(grp0, pv[0], pv[1]) + p_new.astype(BF16).astype(F32) * v_new) / den

    w_len = win_ref.shape[2]
    nw = nwt_ref[0]
    pick = lax.broadcasted_iota(jnp.int32, nw.shape, 1) == b
    new_col = jnp.sum(jnp.where(pick, nw, 0.0), axis=-1, keepdims=True)
    wlane = lax.broadcasted_iota(jnp.int32, (KV_ROW, w_len), 1)
    wout = jnp.where(wlane == w_len - 1, new_col, pltpu.roll(win_ref[0], w_len - 1, axis=1))
    wout_ref[0] = wout
    sw = _dot(qb, wout[:KV_LANES].astype(BF16))
    ew = jnp.exp(sw - jnp.max(sw, axis=-1, keepdims=True))
    o_win = _dot_nt(ew.astype(BF16), wout[KV_LANES:].astype(BF16)) / jnp.sum(ew, axis=-1, keepdims=True)

    gn = jnp.broadcast_to(gn_ref[0], (N_HEADS, LANES))
    head = _mod(row, KV_HEADS) * HPG + _div(row, KV_HEADS)

    def gate(branch):
        return jnp.sum(jnp.where(lane == head * N_BRANCH + branch, gn, 0.0), axis=-1, keepdims=True)

    out = gate(0) * ocmp_ref[0] + gate(1) * o_sel + gate(2) * o_win
    o_ref[0] = jnp.concatenate(
        [jnp.sum(jnp.where((_div(row, KV_HEADS) == v) & (_div(lane, HEAD_DIM) == _mod(row, KV_HEADS)), out, 0.0),
                 axis=0, keepdims=True)
         for v in range(HPG)], axis=1).astype(BF16)


def _sample_attn(page_table, idx, cache, q, gn, new_sel, new_win_t, win_t, ocmp, *, t_pos):
    bsz, n_pages = page_table.shape
    page = cache.shape[-1]
    w_len = win_t.shape[2]
    one = lambda n: pl.BlockSpec((1, 1, n), lambda i, *_: (i, 0, 0))
    win_spec = pl.BlockSpec((1, KV_ROW, w_len), lambda i, *_: (i, 0, 0))
    gs = pltpu.PrefetchScalarGridSpec(
        num_scalar_prefetch=2,
        grid=(bsz,),
        in_specs=[pl.BlockSpec(memory_space=pl.ANY), one(D_ATTN), one(LANES), one(KV_ROW),
                  pl.BlockSpec(new_win_t.shape, lambda i, *_: (0, 0, 0), pipeline_mode=pl.Buffered(1)), win_spec,
                  pl.BlockSpec((1, N_HEADS, LANES), lambda i, *_: (i, 0, 0))],
        out_specs=[one(D_ATTN), win_spec],
        scratch_shapes=[pltpu.VMEM((2, KV_HEADS * TOP_N, 2, KV_LANES, page), F32), pltpu.SemaphoreType.DMA((2,))],
    )
    return pl.pallas_call(
        functools.partial(_sample_attn_body, n_pages=n_pages, page=page, t_pos=t_pos),
        grid_spec=gs,
        out_shape=[jax.ShapeDtypeStruct((bsz, 1, D_ATTN), BF16), jax.ShapeDtypeStruct(win_t.shape, F32)],
        compiler_params=_cparams(1),
        name="sample_attn",
    )(page_table.reshape(-1), idx.reshape(-1), cache, q, gn, new_sel, new_win_t, win_t, ocmp)


def _rope_tables(pos):
    half = HEAD_DIM // 2
    freqs = jnp.power(ROPE_THETA, -jnp.arange(half, dtype=F32) * (2.0 / HEAD_DIM))
    ang = pos.astype(F32)[:, None] * freqs[None, :]
    cos, sin = jnp.cos(ang), jnp.sin(ang)
    reps = LANES // HEAD_DIM
    return (jnp.tile(jnp.concatenate([cos, cos], axis=1), (1, reps)),
            jnp.tile(jnp.concatenate([-sin, sin], axis=1), (1, reps)), cos.T, sin.T)


def _compress_weights(w1k, w2k, pek, w1v, w2v, pev):
    n_j = 2 * KV_HEADS
    wbd = jnp.zeros((CMP_STRIDE, n_j, HEAD_DIM, CMP_RATIO, n_j, CMP_HID), F32)
    for j in range(n_j):
        w1 = w1k if j < KV_HEADS else w1v
        wbd = wbd.at[:, j, :, :, j, :].set(w1.transpose(1, 2, 0, 3))
    wbd = wbd.reshape(CMP_STRIDE, KV_ROW, CMP_RATIO * n_j * CMP_HID).astype(BF16)

    def pe_rows(r):
        k = pek[r * CMP_STRIDE:(r + 1) * CMP_STRIDE]
        v = pev[r * CMP_STRIDE:(r + 1) * CMP_STRIDE]
        return jnp.concatenate([k] * KV_HEADS + [v] * KV_HEADS, axis=1)

    def pair(w2):
        z = jnp.zeros_like(w2)
        return jnp.concatenate([jnp.concatenate([w2, z], axis=1), jnp.concatenate([z, w2], axis=1)],
                               axis=0).astype(BF16)

    return pe_rows(0), pe_rows(1), wbd, pair(w2k), pair(w2v)


def _cmp_to_sel(n_cmp_pad, n_sel_pad, n_cmp, n_sel):
    i = np.arange(n_cmp_pad)[:, None]
    j = np.arange(n_sel_pad)[None, :]
    start = i * CMP_STRIDE
    return (start < j * SEL_BLK + SEL_BLK) & (start + CMP_BLK > j * SEL_BLK) & (i < n_cmp) & (j < n_sel)


def _kv_out(rows_t):
    n, _, t = rows_t.shape
    return rows_t.reshape(1, n, 2, KV_HEADS, HEAD_DIM, t).transpose(0, 1, 5, 2, 3, 4)


def _kv_in(a):
    n, t = a.shape[:2]
    return a.transpose(0, 2, 3, 4, 1).reshape(n, 2, KV_LANES, t)


def kernel(x_prompt, x_sample, cache_cmp_kv, cache_sel_kv, state_win_kv, state_conv, page_table, norm_ffn1, ffn1_w_gate, ffn1_w_up, ffn1_w_down, norm_mix, w_mix_in, conv_w, w_cmpk1, w_cmpk2, pe_cmpk, w_cmpv1, w_cmpv2, pe_cmpv, w_conv_out, w_attn_out, w_mix_out, norm_ffn2, ffn2_w_gate, ffn2_w_up, ffn2_w_down, norm_final):
    depth = norm_ffn1.shape[0]
    assert depth == 1
    bsz, t_len, _ = x_prompt.shape
    dbsz, dseq, _ = x_sample.shape
    n_pages = page_table.shape[1]
    page = cache_cmp_kv.shape[2]
    past = n_pages * page
    w_len = state_win_kv.shape[2]
    assert dseq == 1 and w_len == WINDOW and past >= WINDOW and past % SEL_BLK == 0 and page % SEL_BLK == 0
    assert t_len >= WINDOW and t_len // CMP_STRIDE == LANES

    offs = np.cumsum(IN_SPLITS)[:-1].tolist()
    w_ub, w_uc, w_ux, w_q, w_kvc, w_kvs, w_kvw, w_gn, w_gb = jnp.split(w_mix_in[0], offs, axis=1)
    order = np.asarray(HEAD_ORDER)
    w_q = w_q.reshape(D_MODEL, N_HEADS, HEAD_DIM)[:, order].reshape(D_MODEL, D_ATTN)
    w_gn = jnp.pad(w_gn, ((0, 0), (0, LANES - w_gn.shape[1])))
    w_r = jnp.concatenate([w_uc, w_ux, w_q, w_gn], axis=1).astype(BF16)
    w_r_s = jnp.concatenate([w_uc, w_ux, w_q, w_gn, w_kvs, w_kvw], axis=1).astype(BF16)
    w_t = jnp.concatenate([w_kvc, w_kvs, w_kvw], axis=1).T.astype(BF16)
    w_b = jnp.concatenate([w_ub, w_gb], axis=1).astype(BF16)
    w_ao = w_attn_out[0].reshape(N_HEADS, HEAD_DIM, D_MODEL)[order].reshape(D_ATTN, D_MODEL).astype(BF16)
    w_co = w_conv_out[0].astype(BF16)
    w_mo = w_mix_out[0].astype(BF16)
    cw = _compress_weights(w_cmpk1[0], w_cmpk2[0], pe_cmpk[0], w_cmpv1[0], w_cmpv2[0], pe_cmpv[0])
    ffn1 = (norm_ffn1, ffn1_w_gate[0].astype(BF16), ffn1_w_up[0].astype(BF16), ffn1_w_down[0].astype(BF16))
    ffn2 = (norm_ffn2, ffn2_w_gate[0].astype(BF16), ffn2_w_up[0].astype(BF16), ffn2_w_down[0].astype(BF16))
    g_final = norm_final.reshape(1, D_MODEL)

    xp = _ffn(x_prompt.reshape(bsz * t_len, D_MODEL), *ffn1, g_final, final_norm=False)
    tabs = _rope_tables(jnp.arange(t_len, dtype=jnp.int32))
    uin, q, gn, rct, rst, rwt, kst, kwt = _mixin(xp.reshape(bsz, t_len, D_MODEL), norm_mix, w_r, w_t, tabs,
                                                 rows_out=False)
    kct, vc = _compress_prompt(rct, cw)
    n_chunks = t_len // CMP_STRIDE
    n_sel = t_len // SEL_BLK
    e_t = jnp.asarray(np.arange(LANES)[:, None] == np.arange(t_len)[None, :] // SEL_BLK, BF16)
    c2s_t = jnp.asarray(_cmp_to_sel(n_chunks, LANES, n_chunks - CMP_RATIO + 1, n_sel).T, BF16)
    o = _nsa_prompt(q, gn, kct, vc, kst, kwt, e_t, c2s_t)
    uin = uin.reshape(bsz * t_len, D_CONV)
    xp = _merge(xp, norm_mix, uin, uin, uin, o.reshape(bsz * t_len, D_ATTN), w_b, conv_w[0], w_co, w_ao, w_mo,
                seq_len=t_len)
    y_prompt = _ffn(xp, *ffn2, g_final, final_norm=True).reshape(bsz, t_len, D_MODEL)
    cmp_p = _kv_out(rct)
    sel_p = _kv_out(rst)
    win_p = _kv_out(rwt[:, :, t_len - WINDOW:])
    conv_p = uin.reshape(1, bsz, t_len, D_CONV)[:, :, t_len - (CONV_W - 1):]

    xs = _ffn(x_sample.reshape(dbsz, D_MODEL), *ffn1, g_final, final_norm=False)
    tabs_s = _rope_tables(jnp.full((dbsz,), past, jnp.int32))
    uin_s, q_s, gn_s, rct_s, rst_s, rwt_s, _, _, rs_s, _ = _mixin(xs.reshape(1, dbsz, D_MODEL), norm_mix, w_r_s,
                                                                  w_t, tabs_s, rows_out=True)
    one = lambda a: a.reshape(dbsz, 1, a.shape[-1])
    n_chunks_s = past // CMP_STRIDE
    n_sel_s = past // SEL_BLK + 1
    n_sel_pad = -(-n_sel_s // LANES) * LANES
    c2s_s = jnp.asarray(_cmp_to_sel(n_chunks_s, n_sel_pad, (past + 1) // CMP_STRIDE - CMP_RATIO + 1, n_sel_s), BF16)
    ocmp, idx = _sample_cmp(page_table, _kv_in(cache_cmp_kv[0]), one(q_s), cw, c2s_s, t_pos=past)
    idx = idx[:, :KV_HEADS, :TOP_N]
    win_t = _kv_in(state_win_kv[0]).reshape(dbsz, KV_ROW, w_len)
    o_s, win_s = _sample_attn(page_table, idx, _kv_in(cache_sel_kv[0]), one(q_s), one(gn_s), one(rs_s), rwt_s,
                              win_t, ocmp, t_pos=past)
    uin_s = uin_s.reshape(dbsz, D_CONV)
    xs = _merge(xs, norm_mix, uin_s, state_conv[0, :, 1], state_conv[0, :, 0], o_s.reshape(dbsz, D_ATTN), w_b,
                conv_w[0], w_co, w_ao, w_mo, seq_len=None)
    y_sample = _ffn(xs, *ffn2, g_final, final_norm=True).reshape(dbsz, 1, D_MODEL)
    new_out = lambda rt: rt.reshape(1, 2, KV_HEADS, HEAD_DIM, dbsz).transpose(0, 4, 1, 2, 3)[:, :, None]
    cmp_s = new_out(rct_s)
    sel_s = new_out(rst_s)
    win_s = _kv_out(win_s)
    conv_s = jnp.concatenate([state_conv[0, :, 1:], uin_s[:, None, :]], axis=1)[None]
    return (y_prompt, y_sample, cmp_p, sel_p, win_p, conv_p, cmp_s, sel_s, win_s, conv_s)
```

```python
import functools

import numpy as np
import jax
import jax.numpy as jnp
from jax import lax
from jax.experimental import pallas as pl
from jax.experimental.pallas import tpu as pltpu

F32 = jnp.float32
BF16 = jnp.bfloat16

D_MODEL = 1024
D_CONV = 512
CONV_W = 3
N_HEADS = 8
KV_HEADS = 2
HPG = N_HEADS // KV_HEADS
HEAD_DIM = 64
D_ATTN = N_HEADS * HEAD_DIM
KV_ROW = 2 * KV_HEADS * HEAD_DIM
CMP_STRIDE = 16
CMP_RATIO = 2
CMP_BLK = CMP_STRIDE * CMP_RATIO
CMP_HID = 128
SEL_BLK = 64
TOP_N = 16
WINDOW = 512
N_BRANCH = 3
D_FF = 2816
ROPE_THETA = 10000.0
EPS = 1e-6
NEG = -1e30
BIG = 1e9
GONE = -3e38
IN_SPLITS = (D_CONV, D_CONV, D_CONV, D_ATTN, KV_ROW, KV_ROW, KV_ROW, N_HEADS * N_BRANCH, 2 * D_MODEL)

LANES = 128
KV_LANES = KV_HEADS * HEAD_DIM
assert KV_LANES == LANES and KV_HEADS == 2 and 2 * HEAD_DIM == LANES
HEAD_ORDER = tuple(half * HPG + v for v in range(HPG) for half in range(KV_HEADS))
VMEM_LIMIT = 56 * 1024 * 1024
ROW_TILE = 512
KEY_TILE = 256


def _cparams(n_grid):
    return pltpu.CompilerParams(dimension_semantics=("arbitrary",) * n_grid, vmem_limit_bytes=VMEM_LIMIT)


def _const_spec(shape):
    nd = len(shape)
    return pl.BlockSpec(shape, lambda *_: (0,) * nd, pipeline_mode=pl.Buffered(1))


def _rms(x, g):
    return x * lax.rsqrt(jnp.mean(x * x, axis=-1, keepdims=True) + EPS) * g


def _silu(x):
    return x * jax.nn.sigmoid(x)


def _div(x, n):
    return lax.shift_right_logical(x, int(np.log2(n)))


def _mod(x, n):
    return x & (n - 1)


def _dot(a, b):
    return jnp.dot(a, b, preferred_element_type=F32)


def _dot_nt(a, b):
    return lax.dot_general(a, b, (((1,), (1,)), ((), ())), preferred_element_type=F32)


def _ffn_body(x_ref, g_ref, wg_ref, wu_ref, wd_ref, gf_ref, o_ref, *, final_norm):
    x = x_ref[...]
    h = _rms(x, g_ref[...]).astype(BF16)
    a = (_silu(_dot(h, wg_ref[...])) * _dot(h, wu_ref[...])).astype(BF16)
    y = x + 0.5 * _dot(a, wd_ref[...])
    if final_norm:
        y = _rms(y, gf_ref[...])
    o_ref[...] = y


def _ffn(x, g, wg, wu, wd, gf, *, final_norm):
    m = x.shape[0]
    tm = min(ROW_TILE, m)
    row = pl.BlockSpec((tm, D_MODEL), lambda i: (i, 0))
    return pl.pallas_call(
        functools.partial(_ffn_body, final_norm=final_norm),
        grid=(m // tm,),
        in_specs=[row, _const_spec((1, D_MODEL)), _const_spec(wg.shape), _const_spec(wu.shape),
                  _const_spec(wd.shape), _const_spec((1, D_MODEL))],
        out_specs=row,
        out_shape=jax.ShapeDtypeStruct((m, D_MODEL), F32),
        compiler_params=_cparams(1),
        name="ffn",
    )(x, g, wg, wu, wd, gf)


ROW_COLS = {"uc": (0, 512), "ux": (512, 1024), "q": (1024, 1536), "gn": (1536, 1664), "kvs": (1664, 1920),
            "kvw": (1920, 2176)}


def _rope_tile(z, cos, sin):
    first = _mod(lax.broadcasted_iota(jnp.int32, z.shape, 1), HEAD_DIM) < (HEAD_DIM // 2)
    rot = jnp.where(first, pltpu.roll(z, LANES - HEAD_DIM // 2, axis=1), pltpu.roll(z, HEAD_DIM // 2, axis=1))
    return z * cos + rot * sin


def _rope_rows_t(zt, cos_t, sin_t):
    half = HEAD_DIM // 2
    out = []
    for hh in range(KV_LANES // HEAD_DIM):
        x1 = zt[hh * HEAD_DIM:hh * HEAD_DIM + half]
        x2 = zt[hh * HEAD_DIM + half:(hh + 1) * HEAD_DIM]
        out += [x1 * cos_t - x2 * sin_t, x1 * sin_t + x2 * cos_t]
    return jnp.concatenate(out, axis=0)


def _mixin_body(x_ref, g_ref, wr_ref, wt_ref, cos_ref, sin_ref, cost_ref, sint_ref,
                uin_ref, q_ref, gn_ref, rct_ref, rst_ref, rwt_ref, kst_ref, kwt_ref, *row_refs):
    h = _rms(x_ref[0], g_ref[...]).astype(BF16)
    cos = cos_ref[...]
    sin = sin_ref[...]

    def proj(name):
        lo, hi = ROW_COLS[name]
        return _dot(h, wr_ref[:, lo:hi])

    uin_ref[0] = proj("uc") * proj("ux")
    zq = proj("q")
    scale = HEAD_DIM ** -0.5
    q_ref[0] = jnp.concatenate(
        [_rope_tile(zq[:, v * LANES:(v + 1) * LANES], cos, sin) * scale for v in range(D_ATTN // LANES)],
        axis=1).astype(BF16)
    gn_ref[0] = jax.nn.sigmoid(proj("gn"))
    cos_t = cost_ref[...]
    sin_t = sint_ref[...]
    for i, (f_ref, b_ref) in enumerate(((rct_ref, None), (rst_ref, kst_ref), (rwt_ref, kwt_ref))):
        zt = _dot_nt(wt_ref[i * KV_ROW:(i + 1) * KV_ROW, :], h)
        rows_t = jnp.concatenate([_rope_rows_t(zt[:KV_LANES], cos_t, sin_t), zt[KV_LANES:]], axis=0)
        f_ref[0] = rows_t
        if b_ref is not None:
            b_ref[0] = rows_t.astype(BF16)
    for name, r_ref in zip(("kvs", "kvw"), row_refs):
        z = proj(name)
        r_ref[0] = jnp.concatenate([_rope_tile(z[:, :KV_LANES], cos, sin), z[:, KV_LANES:]], axis=1)


def _mixin(x, g, wr, wt, tabs, *, rows_out):
    b, t, _ = x.shape
    tm = min(ROW_TILE, t)
    row = lambda n: pl.BlockSpec((1, tm, n), lambda i, j: (i, j, 0))
    col = pl.BlockSpec((1, KV_ROW, tm), lambda i, j: (i, 0, j))
    tab = pl.BlockSpec((tm, LANES), lambda i, j: (j, 0))
    tab_t = pl.BlockSpec((HEAD_DIM // 2, tm), lambda i, j: (0, j))
    rsd = lambda n, dt: jax.ShapeDtypeStruct((b, t, n), dt)
    csd = lambda dt: jax.ShapeDtypeStruct((b, KV_ROW, t), dt)
    extra = 2 if rows_out else 0
    return pl.pallas_call(
        _mixin_body,
        grid=(b, t // tm),
        in_specs=[row(D_MODEL), _const_spec((1, D_MODEL)), _const_spec(wr.shape), _const_spec(wt.shape),
                  tab, tab, tab_t, tab_t],
        out_specs=[row(D_CONV), row(D_ATTN), row(LANES), col, col, col, col, col] + [row(KV_ROW)] * extra,
        out_shape=[rsd(D_CONV, F32), rsd(D_ATTN, BF16), rsd(LANES, F32), csd(F32), csd(F32), csd(F32), csd(BF16),
                   csd(BF16)] + [rsd(KV_ROW, F32)] * extra,
        compiler_params=_cparams(2),
        name="mix_in",
    )(x, g, wr, wt, *tabs)


def _compress(rows_ref, n_chunks, pe0_ref, pe1_ref, wbd_ref, w2k_ref, w2v_ref):
    half = KV_ROW * CMP_RATIO
    acc0 = jnp.zeros((n_chunks, half), F32)
    acc1 = jnp.zeros((n_chunks, half), F32)
    for s in range(CMP_STRIDE):
        a = jnp.concatenate([rows_ref[kv, pl.ds(s, n_chunks, stride=CMP_STRIDE), :] for kv in range(2)], axis=1)
        acc0 = acc0 + _dot((a + pe0_ref[s:s + 1, :]).astype(BF16), wbd_ref[s, :, :half])
        acc1 = acc1 + _dot((a + pe1_ref[s:s + 1, :]).astype(BF16), wbd_ref[s, :, half:])
    hid = acc0 + pltpu.roll(acc1, n_chunks - 1, axis=0)
    act = _silu(hid).astype(BF16)
    keep = lax.broadcasted_iota(jnp.int32, (n_chunks, LANES), 0) < n_chunks - 1
    kc = jnp.where(keep, _dot(act[:, :half // 2], w2k_ref[...]), 0.0)
    vc = jnp.where(keep, _dot(act[:, half // 2:], w2v_ref[...]), 0.0)
    return kc, vc


def _compress_prompt_body(xt_ref, pe0_ref, pe1_ref, wbd_ref, w2k_ref, w2v_ref, kct_ref, vc_ref, rows_ref,
                          *, n_chunks, t_len):
    for kv in range(2):
        for c in range(t_len // LANES):
            rows_ref[kv, c * LANES:(c + 1) * LANES, :] = (
                xt_ref[0, kv * KV_LANES:(kv + 1) * KV_LANES, c * LANES:(c + 1) * LANES].T)
    kc, vc = _compress(rows_ref, n_chunks, pe0_ref, pe1_ref, wbd_ref, w2k_ref, w2v_ref)
    kct_ref[0] = kc.T.astype(BF16)
    vc_ref[0] = vc.astype(BF16)


def _compress_prompt(rct, cw):
    b, _, t = rct.shape
    n_chunks = t // CMP_STRIDE
    out = pl.BlockSpec((1, n_chunks, LANES), lambda i: (i, 0, 0))
    return pl.pallas_call(
        functools.partial(_compress_prompt_body, n_chunks=n_chunks, t_len=t),
        grid=(b,),
        in_specs=[pl.BlockSpec((1, KV_ROW, t), lambda i: (i, 0, 0))] + [_const_spec(w.shape) for w in cw],
        out_specs=[out, out],
        out_shape=[jax.ShapeDtypeStruct((b, n_chunks, LANES), BF16)] * 2,
        scratch_shapes=[pltpu.VMEM((2, t, LANES), F32)],
        compiler_params=_cparams(1),
        name="compress_prompt",
    )(rct, *cw)


def _softmax_rows(s):
    e = jnp.exp(s - jnp.max(s, axis=-1, keepdims=True))
    return e / jnp.sum(e, axis=-1, keepdims=True)


def _split_hi_lo(x):
    hi = x.astype(BF16)
    return hi, (x - hi.astype(F32)).astype(BF16)


def _nsa_prompt_body(q_ref, gn_ref, kct_ref, vc_ref, ks_ref, kw_ref, e_ref, c2st_ref, o_ref,
                     qaug_ref, p_ref, m_ref, l_ref, acc_ref, alpha_ref, out_ref, *, tq, kt, t_len):
    n_slots = N_HEADS
    rows = n_slots * tq
    n_sel = t_len // SEL_BLK
    start = pl.program_id(1) * tq
    lane = lax.broadcasted_iota(jnp.int32, (tq, LANES), 1)
    t_pos = start + lax.broadcasted_iota(jnp.int32, (tq, LANES), 0)
    low = lane < HEAD_DIM
    gn = gn_ref[0]
    slot_rows = [slice(slot * tq, (slot + 1) * tq) for slot in range(n_slots)]

    def gate(slot, branch):
        v, half = divmod(slot, KV_HEADS)
        c = (half * HPG + v) * N_BRANCH + branch
        return gn[:, c:c + 1]

    q = q_ref[0].astype(F32)
    for slot, rs in enumerate(slot_rows):
        v, half = divmod(slot, KV_HEADS)
        qv = q[:, v * LANES:(v + 1) * LANES]
        qaug_ref[rs, :LANES] = (jnp.where(low, qv, 0.0) if half == 0 else jnp.where(low, 0.0, qv)).astype(BF16)

    cmp_ok = lane * CMP_STRIDE + (CMP_BLK - 1) <= t_pos
    s_c = _dot(qaug_ref[:, :LANES], kct_ref[0])
    p_c = jnp.concatenate(
        [jnp.where(cmp_ok, _softmax_rows(jnp.where(cmp_ok, s_c[rs], NEG)), 0.0) for rs in slot_rows], axis=0)
    o_c = _dot(p_c.astype(BF16), vc_ref[0])
    for slot, rs in enumerate(slot_rows):
        out_ref[rs, :] = gate(slot, 0) * o_c[rs]

    blk = lax.broadcasted_iota(jnp.int32, (n_sel, tq), 0)
    cur = _div(start + lax.broadcasted_iota(jnp.int32, (n_sel, tq), 1), SEL_BLK)
    valid = blk <= cur
    forced = valid & ((blk == 0) | (blk == cur) | (blk == cur - 1))
    for half in range(KV_HEADS):
        psum = p_c[slot_rows[half]]
        for v in range(1, HPG):
            psum = psum + p_c[slot_rows[v * KV_HEADS + half]]
        hi, lo = _split_hi_lo(psum)
        imp = (_dot_nt(c2st_ref[...], hi) + _dot_nt(c2st_ref[...], lo))[:n_sel]
        score = jnp.where(forced, BIG, jnp.where(valid, imp, -BIG))
        rank = jnp.zeros((n_sel, tq), F32)
        for i in range(n_sel):
            r = score[i:i + 1, :]
            tie = jnp.where(blk > i, 1.0, 0.0)
            rank = rank + jnp.where(r > score, 1.0, jnp.where(r == score, tie, 0.0))
        pen_t = jnp.concatenate([jnp.where(rank >= TOP_N, NEG, 0.0), jnp.zeros((LANES - n_sel, tq), F32)], axis=0)
        pen = pen_t.T.astype(BF16)
        for v in range(HPG):
            qaug_ref[slot_rows[v * KV_HEADS + half], LANES:] = pen

    def reset():
        m_ref[...] = jnp.full((rows, LANES), NEG, F32)
        l_ref[...] = jnp.zeros((rows, LANES), F32)
        acc_ref[...] = jnp.zeros((rows, LANES), F32)

    def tile_step(kv_ref, koff, w, select, masked, window):
        kv = kv_ref[0, :, pl.ds(koff, w)]
        if select:
            s_all = _dot(qaug_ref[...], jnp.concatenate([kv[:KV_LANES], e_ref[:, pl.ds(koff, w)]], axis=0))
        else:
            s_all = _dot(qaug_ref[:, :LANES], kv[:KV_LANES])
        if masked:
            rel = (lax.broadcasted_iota(jnp.int32, (tq, w), 1) - lax.broadcasted_iota(jnp.int32, (tq, w), 0)
                   + (koff - start))
            ok = (rel <= 0) & (rel > -WINDOW) if window else rel <= 0
        for rs in slot_rows:
            s = jnp.where(ok, s_all[rs], NEG) if masked else s_all[rs]
            m_prev = m_ref[rs, :]
            m_new = jnp.maximum(m_prev, jnp.max(s, axis=-1, keepdims=True))
            alpha_ref[rs, :] = jnp.exp(m_prev - m_new)
            p_ref[rs, :w] = jnp.exp(s - jnp.concatenate([m_new] * (w // LANES), axis=1)).astype(BF16)
            m_ref[rs, :] = m_new
        pv = _dot_nt(p_ref[:, :w], jnp.concatenate([kv[KV_LANES:], jnp.ones((KV_LANES, w), BF16)], axis=0))
        acc_ref[...] = alpha_ref[...] * acc_ref[...] + pv[:, :LANES]
        l_ref[...] = alpha_ref[...] * l_ref[...] + pv[:, LANES:]

    def finish(branch):
        for slot, rs in enumerate(slot_rows):
            out_ref[rs, :] = out_ref[rs, :] + gate(slot, branch) * (acc_ref[rs, :] / l_ref[rs, :])

    reset()
    n_full = start // kt

    def sel_body(j, carry):
        tile_step(ks_ref, pl.multiple_of(j * 2 * kt, 2 * kt), 2 * kt, True, False, False)
        return carry

    lax.fori_loop(0, n_full // 2, sel_body, 0)

    @pl.when(n_full % 2 == 1)
    def _():
        tile_step(ks_ref, pl.multiple_of(start - kt, kt), 2 * kt, True, True, False)

    @pl.when(n_full % 2 == 0)
    def _():
        tile_step(ks_ref, pl.multiple_of(start, kt), kt, True, True, False)
    finish(1)

    reset()
    n_win = WINDOW // kt
    for n_before in range(n_win + 1):
        @pl.when(start >= WINDOW if n_before == n_win else start == n_before * kt)
        def _():
            tile_step(kw_ref, pl.multiple_of(start - n_before * kt, kt), (n_before + 1) * kt, False, True, True)
    finish(2)

    o_ref[0] = jnp.concatenate(
        [jnp.where(low, out_ref[slot_rows[2 * v], :], out_ref[slot_rows[2 * v + 1], :]) for v in range(HPG)],
        axis=1).astype(BF16)


def _nsa_prompt(q, gn, kct, vc, kst, kwt, e_t, c2s_t):
    b, t, _ = q.shape
    tq = kt = KEY_TILE
    assert WINDOW % kt == 0 and t % kt == 0
    n_cmp = vc.shape[1]
    rows = N_HEADS * tq
    tile = lambda n: pl.BlockSpec((1, tq, n), lambda i, j: (i, j, 0))
    per_b = lambda r, n: pl.BlockSpec((1, r, n), lambda i, j: (i, 0, 0))
    rows_f32 = pltpu.VMEM((rows, LANES), F32)
    return pl.pallas_call(
        functools.partial(_nsa_prompt_body, tq=tq, kt=kt, t_len=t),
        grid=(b, t // tq),
        in_specs=[tile(D_ATTN), tile(LANES), per_b(LANES, n_cmp), per_b(n_cmp, LANES), per_b(KV_ROW, t),
                  per_b(KV_ROW, t), _const_spec(e_t.shape), _const_spec(c2s_t.shape)],
        out_specs=tile(D_ATTN),
        out_shape=jax.ShapeDtypeStruct((b, t, D_ATTN), BF16),
        scratch_shapes=[pltpu.VMEM((rows, 2 * LANES), BF16), pltpu.VMEM((rows, max(2 * kt, WINDOW + kt)), BF16), rows_f32, rows_f32,
                        rows_f32, rows_f32, rows_f32],
        compiler_params=_cparams(2),
        name="nsa_prompt",
    )(q, gn, kct, vc, kst, kwt, e_t, c2s_t)


def _merge_body(x_ref, g_ref, u_ref, um1_ref, um2_ref, o_ref, wb_ref, cw_ref, wco_ref, wao_ref, wmo_ref,
                y_ref, *, tm, tiles_per_seq):
    x = x_ref[...]
    h = _rms(x, g_ref[...]).astype(BF16)
    ub = _dot(h, wb_ref[:, :D_CONV])
    u = u_ref[...]
    if tiles_per_seq is None:
        um1 = um1_ref[...]
        um2 = um2_ref[...]
    else:
        row = lax.broadcasted_iota(jnp.int32, (tm, D_CONV), 0)
        keep = jnp.where(pl.program_id(0) % tiles_per_seq == 0, 0.0, 1.0)
        prev = um1_ref[...] * keep
        um1 = jnp.where(row == 0, prev[7:8, :], pltpu.roll(u, 1, axis=0))
        um2 = jnp.where(row == 0, prev[6:7, :], jnp.where(row == 1, prev[7:8, :], pltpu.roll(u, 2, axis=0)))
    conv = cw_ref[0:1, :] * um2 + cw_ref[1:2, :] * um1 + cw_ref[2:3, :] * u
    y_c = _dot((ub * conv).astype(BF16), wco_ref[...])
    y_a = _dot(o_ref[...], wao_ref[...])
    g_c = jax.nn.sigmoid(_dot(h, wb_ref[:, D_CONV:D_CONV + D_MODEL]))
    g_a = jax.nn.sigmoid(_dot(h, wb_ref[:, D_CONV + D_MODEL:]))
    y_ref[...] = x + _dot((g_c * y_c + g_a * y_a).astype(BF16), wmo_ref[...])


def _merge(x, g, uin, um1, um2, o, wb, cw, wco, wao, wmo, *, seq_len):
    m = x.shape[0]
    tm = min(ROW_TILE, m)
    row = lambda n: pl.BlockSpec((tm, n), lambda i: (i, 0))
    if seq_len is None:
        prev_specs = [row(D_CONV), row(D_CONV)]
        tiles_per_seq = None
    else:
        halo = pl.BlockSpec((8, D_CONV), lambda i: (jnp.maximum(i * (tm // 8) - 1, 0), 0))
        prev_specs = [halo, halo]
        tiles_per_seq = seq_len // tm
    return pl.pallas_call(
        functools.partial(_merge_body, tm=tm, tiles_per_seq=tiles_per_seq),
        grid=(m // tm,),
        in_specs=[row(D_MODEL), _const_spec((1, D_MODEL)), row(D_CONV)] + prev_specs +
                 [row(D_ATTN), _const_spec(wb.shape), _const_spec(cw.shape), _const_spec(wco.shape),
                  _const_spec(wao.shape), _const_spec(wmo.shape)],
        out_specs=row(D_MODEL),
        out_shape=jax.ShapeDtypeStruct((m, D_MODEL), F32),
        compiler_params=_cparams(1),
        name="merge",
    )(x, g, uin, um1, um2, o, wb, cw, wco, wao, wmo)


def _expand_q(q_ref):
    row = lax.broadcasted_iota(jnp.int32, (N_HEADS, LANES), 0)
    lane = lax.broadcasted_iota(jnp.int32, (N_HEADS, LANES), 1)
    q = q_ref[0].astype(F32)
    qe = jnp.zeros((N_HEADS, LANES), F32)
    for v in range(HPG):
        qv = jnp.broadcast_to(q[:, v * LANES:(v + 1) * LANES], (N_HEADS, LANES))
        qe = jnp.where((_div(row, KV_HEADS) == v) & (_div(lane, HEAD_DIM) == _mod(row, KV_HEADS)), qv, qe)
    return qe


def _sample_cmp_body(pt_ref, cache_ref, q_ref, pe0_ref, pe1_ref, wbd_ref, w2k_ref, w2v_ref, c2s_ref,
                     ocmp_ref, idx_ref, buf_ref, rows_ref, sem_ref, *, n_pages, page, t_pos):
    b = pl.program_id(0)
    n_b = pl.num_programs(0)
    n_chunks = n_pages * page // CMP_STRIDE
    n_sel = t_pos // SEL_BLK + 1

    def page_copy(bb, p, slot):
        return pltpu.make_async_copy(cache_ref.at[pt_ref[bb * n_pages + p]], buf_ref.at[slot, p], sem_ref.at[slot])

    def start_all(bb, slot):
        for p in range(n_pages):
            page_copy(bb, p, slot).start()

    @pl.when(b == 0)
    def _():
        start_all(0, 0)

    @pl.when(b + 1 < n_b)
    def _():
        start_all(b + 1, (b + 1) % 2)

    slot = b % 2
    for p in range(n_pages):
        page_copy(b, p, slot).wait()

    group = 8
    for p0 in range(0, n_pages, group):
        for kv in range(2):
            wide = jnp.concatenate([buf_ref[slot, p0 + i, kv] for i in range(group)], axis=1)
            rows_ref[kv, p0 * page:(p0 + group) * page, :] = wide.T
    kc, vc = _compress(rows_ref, n_chunks, pe0_ref, pe1_ref, wbd_ref, w2k_ref, w2v_ref)

    lane = lax.broadcasted_iota(jnp.int32, (N_HEADS, LANES), 1)
    qe = _expand_q(q_ref)
    cidx = lax.broadcasted_iota(jnp.int32, (N_HEADS, n_chunks), 1)
    cmp_ok = cidx * CMP_STRIDE + (CMP_BLK - 1) <= t_pos
    p = jnp.where(cmp_ok, _softmax_rows(jnp.where(cmp_ok, _dot_nt(qe.astype(BF16), kc.astype(BF16)), NEG)), 0.0)
    ocmp_ref[0] = _dot(p.astype(BF16), vc.astype(BF16))

    hi, lo = _split_hi_lo(p)
    imp_h = _dot(hi, c2s_ref[...]) + _dot(lo, c2s_ref[...])
    n_lanes = c2s_ref.shape[1]
    grow = lax.broadcasted_iota(jnp.int32, (N_HEADS, n_lanes), 0)
    glane = lax.broadcasted_iota(jnp.int32, (N_HEADS, n_lanes), 1)
    imp = jnp.zeros((N_HEADS, n_lanes), F32)
    for g in range(KV_HEADS):
        tot = jnp.sum(jnp.where(_mod(grow, KV_HEADS) == g, imp_h, 0.0), axis=0, keepdims=True)
        imp = jnp.where(grow == g, jnp.broadcast_to(tot, imp.shape), imp)
    cur = t_pos // SEL_BLK
    forced = (glane == 0) | (glane == cur) | (glane == cur - 1)
    score = jnp.where(glane < n_sel, jnp.where(forced, BIG, imp), GONE)
    flane = glane.astype(F32)
    picked = jnp.zeros((N_HEADS, LANES), jnp.int32)
    for k in range(TOP_N):
        best = jnp.max(score, axis=-1, keepdims=True)
        where_best = jnp.min(jnp.where(score == best, flane, float(n_lanes)), axis=-1, keepdims=True)
        picked = jnp.where(lane == k, where_best.astype(jnp.int32), picked)
        score = jnp.where(flane == where_best, GONE, score)
    idx_ref[0] = picked


def _sample_cmp(page_table, cache, q, cw, c2s, *, t_pos):
    bsz, n_pages = page_table.shape
    page = cache.shape[-1]
    assert page == LANES
    gs = pltpu.PrefetchScalarGridSpec(
        num_scalar_prefetch=1,
        grid=(bsz,),
        in_specs=[pl.BlockSpec(memory_space=pl.ANY), pl.BlockSpec((1, 1, D_ATTN), lambda i, pt: (i, 0, 0))] +
                 [pl.BlockSpec(w.shape, (lambda nd: lambda i, pt: (0,) * nd)(w.ndim), pipeline_mode=pl.Buffered(1))
                  for w in (*cw, c2s)],
        out_specs=[pl.BlockSpec((1, N_HEADS, LANES), lambda i, pt: (i, 0, 0))] * 2,
        scratch_shapes=[pltpu.VMEM((2, n_pages, 2, KV_LANES, page), F32),
                        pltpu.VMEM((2, n_pages * page, KV_LANES), F32), pltpu.SemaphoreType.DMA((2,))],
    )
    return pl.pallas_call(
        functools.partial(_sample_cmp_body, n_pages=n_pages, page=page, t_pos=t_pos),
        grid_spec=gs,
        out_shape=[jax.ShapeDtypeStruct((bsz, N_HEADS, LANES), F32),
                   jax.ShapeDtypeStruct((bsz, N_HEADS, LANES), jnp.int32)],
        compiler_params=_cparams(1),
        name="sample_cmp",
    )(page_table.reshape(-1), cache, q, *cw, c2s)


def _sample_attn_body(pt_ref, idx_ref, cache_ref, q_ref, gn_ref, ns_ref, nwt_ref, win_ref, ocmp_ref,
                      o_ref, wout_ref, buf_ref, sem_ref, *, n_pages, page, t_pos):
    b = pl.program_id(0)
    n_b = pl.num_programs(0)
    halves = page // SEL_BLK
    new_blk = t_pos // SEL_BLK
    n_slot = KV_HEADS * TOP_N

    def blk_of(bb, k):
        return idx_ref[bb * n_slot + k]

    def page_copy(bb, k, slot):
        j = jnp.minimum(blk_of(bb, k), new_blk - 1)
        return pltpu.make_async_copy(cache_ref.at[pt_ref[bb * n_pages + j // halves]], buf_ref.at[slot, k],
                                     sem_ref.at[slot])

    def start_all(bb, slot):
        for k in range(n_slot):
            page_copy(bb, k, slot).start()

    @pl.when(b == 0)
    def _():
        start_all(0, 0)

    @pl.when(b + 1 < n_b)
    def _():
        start_all(b + 1, (b + 1) % 2)

    slot = b % 2
    for k in range(n_slot):
        page_copy(b, k, slot).wait()

    row = lax.broadcasted_iota(jnp.int32, (N_HEADS, LANES), 0)
    lane = lax.broadcasted_iota(jnp.int32, (N_HEADS, LANES), 1)
    grp0 = _mod(row, KV_HEADS) == 0
    qe = _expand_q(q_ref)
    qb = qe.astype(BF16)

    n_keys = TOP_N * page
    klane = lax.broadcasted_iota(jnp.int32, (N_HEADS, n_keys), 1)
    s_g = []
    for g in range(KV_HEADS):
        kt_g = jnp.concatenate([buf_ref[slot, g * TOP_N + k, 0] for k in range(TOP_N)], axis=1).astype(BF16)
        pen = jnp.full((N_HEADS, n_keys), NEG, F32)
        for k in range(TOP_N):
            j = blk_of(b, g * TOP_N + k)
            lo = k * page + (j % halves) * SEL_BLK
            seen = jnp.where(j >= new_blk, NEG, 0.0)
            pen = jnp.where((klane >= lo) & (klane < lo + SEL_BLK), seen, pen)
        s_g.append(_dot(qb, kt_g) + pen)
    grp0_k = _mod(lax.broadcasted_iota(jnp.int32, (N_HEADS, n_keys), 0), KV_HEADS) == 0
    s = jnp.where(grp0_k, s_g[0], s_g[1])
    r_new = ns_ref[0].astype(BF16).astype(F32)
    k_new, v_new = r_new[:, :KV_LANES], r_new[:, KV_LANES:]
    s_new = jnp.sum(qe * k_new, axis=-1, keepdims=True)
    m = jnp.maximum(jnp.max(s, axis=-1, keepdims=True), s_new)
    p = jnp.exp(s - m)
    p_new = jnp.exp(s_new - m)
    den = jnp.sum(p, axis=-1, keepdims=True) + p_new
    pb = p.astype(BF16)
    pv = []
    for g in range(KV_HEADS):
        vt_g = jnp.concatenate([buf_ref[slot, g * TOP_N + k, 1] for k in range(TOP_N)], axis=1).astype(BF16)
        pv.append(_dot_nt(pb, vt_g))
    o_sel = (jnp.where(grp0, pv[0], pv[1]) + p_new.astype(BF16).astype(F32) * v_new) / den

    w_len = win_ref.shape[2]
    nw = nwt_ref[0]
    pick = lax.broadcasted_iota(jnp.int32, nw.shape, 1) == b
    new_col = jnp.sum(jnp.where(pick, nw, 0.0), axis=-1, keepdims=True)
    wlane = lax.broadcasted_iota(jnp.int32, (KV_ROW, w_len), 1)
    wout = jnp.where(wlane == w_len - 1, new_col, pltpu.roll(win_ref[0], w_len - 1, axis=1))
    wout_ref[0] = wout
    sw = _dot(qb, wout[:KV_LANES].astype(BF16))
    ew = jnp.exp(sw - jnp.max(sw, axis=-1, keepdims=True))
    o_win = _dot_nt(ew.astype(BF16), wout[KV_LANES:].astype(BF16)) / jnp.sum(ew, axis=-1, keepdims=True)

    gn = jnp.broadcast_to(gn_ref[0], (N_HEADS, LANES))
    head = _mod(row, KV_HEADS) * HPG + _div(row, KV_HEADS)

    def gate(branch):
        return jnp.sum(jnp.where(lane == head * N_BRANCH + branch, gn, 0.0), axis=-1, keepdims=True)

    out = gate(0) * ocmp_ref[0] + gate(1) * o_sel + gate(2) * o_win
    o_ref[0] = jnp.concatenate(
        [jnp.sum(jnp.where((_div(row, KV_HEADS) == v) & (_div(lane, HEAD_DIM) == _mod(row, KV_HEADS)), out, 0.0),
                 axis=0, keepdims=True)
         for v in range(HPG)], axis=1).astype(BF16)


def _sample_attn(page_table, idx, cache, q, gn, new_sel, new_win_t, win_t, ocmp, *, t_pos):
    bsz, n_pages = page_table.shape
    page = cache.shape[-1]
    w_len = win_t.shape[2]
    one = lambda n: pl.BlockSpec((1, 1, n), lambda i, *_: (i, 0, 0))
    win_spec = pl.BlockSpec((1, KV_ROW, w_len), lambda i, *_: (i, 0, 0))
    gs = pltpu.PrefetchScalarGridSpec(
        num_scalar_prefetch=2,
        grid=(bsz,),
        in_specs=[pl.BlockSpec(memory_space=pl.ANY), one(D_ATTN), one(LANES), one(KV_ROW),
                  pl.BlockSpec(new_win_t.shape, lambda i, *_: (0, 0, 0), pipeline_mode=pl.Buffered(1)), win_spec,
                  pl.BlockSpec((1, N_HEADS, LANES), lambda i, *_: (i, 0, 0))],
        out_specs=[one(D_ATTN), win_spec],
        scratch_shapes=[pltpu.VMEM((2, KV_HEADS * TOP_N, 2, KV_LANES, page), F32), pltpu.SemaphoreType.DMA((2,))],
    )
    return pl.pallas_call(
        functools.partial(_sample_attn_body, n_pages=n_pages, page=page, t_pos=t_pos),
        grid_spec=gs,
        out_shape=[jax.ShapeDtypeStruct((bsz, 1, D_ATTN), BF16), jax.ShapeDtypeStruct(win_t.shape, F32)],
        compiler_params=_cparams(1),
        name="sample_attn",
    )(page_table.reshape(-1), idx.reshape(-1), cache, q, gn, new_sel, new_win_t, win_t, ocmp)


def _rope_tables(pos):
    half = HEAD_DIM // 2
    freqs = jnp.power(ROPE_THETA, -jnp.arange(half, dtype=F32) * (2.0 / HEAD_DIM))
    ang = pos.astype(F32)[:, None] * freqs[None, :]
    cos, sin = jnp.cos(ang), jnp.sin(ang)
    reps = LANES // HEAD_DIM
    return (jnp.tile(jnp.concatenate([cos, cos], axis=1), (1, reps)),
            jnp.tile(jnp.concatenate([-sin, sin], axis=1), (1, reps)), cos.T, sin.T)


def _compress_weights(w1k, w2k, pek, w1v, w2v, pev):
    n_j = 2 * KV_HEADS
    wbd = jnp.zeros((CMP_STRIDE, n_j, HEAD_DIM, CMP_RATIO, n_j, CMP_HID), F32)
    for j in range(n_j):
        w1 = w1k if j < KV_HEADS else w1v
        wbd = wbd.at[:, j, :, :, j, :].set(w1.transpose(1, 2, 0, 3))
    wbd = wbd.reshape(CMP_STRIDE, KV_ROW, CMP_RATIO * n_j * CMP_HID).astype(BF16)

    def pe_rows(r):
        k = pek[r * CMP_STRIDE:(r + 1) * CMP_STRIDE]
        v = pev[r * CMP_STRIDE:(r + 1) * CMP_STRIDE]
        return jnp.concatenate([k] * KV_HEADS + [v] * KV_HEADS, axis=1)

    def pair(w2):
        z = jnp.zeros_like(w2)
        return jnp.concatenate([jnp.concatenate([w2, z], axis=1), jnp.concatenate([z, w2], axis=1)],
                               axis=0).astype(BF16)

    return pe_rows(0), pe_rows(1), wbd, pair(w2k), pair(w2v)


def _cmp_to_sel(n_cmp_pad, n_sel_pad, n_cmp, n_sel):
    i = np.arange(n_cmp_pad)[:, None]
    j = np.arange(n_sel_pad)[None, :]
    start = i * CMP_STRIDE
    return (start < j * SEL_BLK + SEL_BLK) & (start + CMP_BLK > j * SEL_BLK) & (i < n_cmp) & (j < n_sel)


def _kv_out(rows_t):
    n, _, t = rows_t.shape
    return rows_t.reshape(1, n, 2, KV_HEADS, HEAD_DIM, t).transpose(0, 1, 5, 2, 3, 4)


def _kv_in(a):
    n, t = a.shape[:2]
    return a.transpose(0, 2, 3, 4, 1).reshape(n, 2, KV_LANES, t)


def kernel(x_prompt, x_sample, cache_cmp_kv, cache_sel_kv, state_win_kv, state_conv, page_table, norm_ffn1, ffn1_w_gate, ffn1_w_up, ffn1_w_down, norm_mix, w_mix_in, conv_w, w_cmpk1, w_cmpk2, pe_cmpk, w_cmpv1, w_cmpv2, pe_cmpv, w_conv_out, w_attn_out, w_mix_out, norm_ffn2, ffn2_w_gate, ffn2_w_up, ffn2_w_down, norm_final):
    depth = norm_ffn1.shape[0]
    assert depth == 1
    bsz, t_len, _ = x_prompt.shape
    dbsz, dseq, _ = x_sample.shape
    n_pages = page_table.shape[1]
    page = cache_cmp_kv.shape[2]
    past = n_pages * page
    w_len = state_win_kv.shape[2]
    assert dseq == 1 and w_len == WINDOW and past >= WINDOW and past % SEL_BLK == 0 and page % SEL_BLK == 0
    assert t_len >= WINDOW and t_len // CMP_STRIDE == LANES

    offs = np.cumsum(IN_SPLITS)[:-1].tolist()
    w_ub, w_uc, w_ux, w_q, w_kvc, w_kvs, w_kvw, w_gn, w_gb = jnp.split(w_mix_in[0], offs, axis=1)
    order = np.asarray(HEAD_ORDER)
    w_q = w_q.reshape(D_MODEL, N_HEADS, HEAD_DIM)[:, order].reshape(D_MODEL, D_ATTN)
    w_gn = jnp.pad(w_gn, ((0, 0), (0, LANES - w_gn.shape[1])))
    w_r = jnp.concatenate([w_uc, w_ux, w_q, w_gn], axis=1).astype(BF16)
    w_r_s = jnp.concatenate([w_uc, w_ux, w_q, w_gn, w_kvs, w_kvw], axis=1).astype(BF16)
    w_t = jnp.concatenate([w_kvc, w_kvs, w_kvw], axis=1).T.astype(BF16)
    w_b = jnp.concatenate([w_ub, w_gb], axis=1).astype(BF16)
    w_ao = w_attn_out[0].reshape(N_HEADS, HEAD_DIM, D_MODEL)[order].reshape(D_ATTN, D_MODEL).astype(BF16)
    w_co = w_conv_out[0].astype(BF16)
    w_mo = w_mix_out[0].astype(BF16)
    cw = _compress_weights(w_cmpk1[0], w_cmpk2[0], pe_cmpk[0], w_cmpv1[0], w_cmpv2[0], pe_cmpv[0])
    ffn1 = (norm_ffn1, ffn1_w_gate[0].astype(BF16), ffn1_w_up[0].astype(BF16), ffn1_w_down[0].astype(BF16))
    ffn2 = (norm_ffn2, ffn2_w_gate[0].astype(BF16), ffn2_w_up[0].astype(BF16), ffn2_w_down[0].astype(BF16))
    g_final = norm_final.reshape(1, D_MODEL)

    xp = _ffn(x_prompt.reshape(bsz * t_len, D_MODEL), *ffn1, g_final, final_norm=False)
    tabs = _rope_tables(jnp.arange(t_len, dtype=jnp.int32))
    uin, q, gn, rct, rst, rwt, kst, kwt = _mixin(xp.reshape(bsz, t_len, D_MODEL), norm_mix, w_r, w_t, tabs,
                                                 rows_out=False)
    kct, vc = _compress_prompt(rct, cw)
    n_chunks = t_len // CMP_STRIDE
    n_sel = t_len // SEL_BLK
    e_t = jnp.asarray(np.arange(LANES)[:, None] == np.arange(t_len)[None, :] // SEL_BLK, BF16)
    c2s_t = jnp.asarray(_cmp_to_sel(n_chunks, LANES, n_chunks - CMP_RATIO + 1, n_sel).T, BF16)
    o = _nsa_prompt(q, gn, kct, vc, kst, kwt, e_t, c2s_t)
    uin = uin.reshape(bsz * t_len, D_CONV)
    xp = _merge(xp, norm_mix, uin, uin, uin, o.reshape(bsz * t_len, D_ATTN), w_b, conv_w[0], w_co, w_ao, w_mo,
                seq_len=t_len)
    y_prompt = _ffn(xp, *ffn2, g_final, final_norm=True).reshape(bsz, t_len, D_MODEL)
    cmp_p = _kv_out(rct)
    sel_p = _kv_out(rst)
    win_p = _kv_out(rwt[:, :, t_len - WINDOW:])
    conv_p = uin.reshape(1, bsz, t_len, D_CONV)[:, :, t_len - (CONV_W - 1):]

    xs = _ffn(x_sample.reshape(dbsz, D_MODEL), *ffn1, g_final, final_norm=False)
    tabs_s = _rope_tables(jnp.full((dbsz,), past, jnp.int32))
    uin_s, q_s, gn_s, rct_s, rst_s, rwt_s, _, _, rs_s, _ = _mixin(xs.reshape(1, dbsz, D_MODEL), norm_mix, w_r_s,
                                                                  w_t, tabs_s, rows_out=True)
    one = lambda a: a.reshape(dbsz, 1, a.shape[-1])
    n_chunks_s = past // CMP_STRIDE
    n_sel_s = past // SEL_BLK + 1
    n_sel_pad = -(-n_sel_s // LANES) * LANES
    c2s_s = jnp.asarray(_cmp_to_sel(n_chunks_s, n_sel_pad, (past + 1) // CMP_STRIDE - CMP_RATIO + 1, n_sel_s), BF16)
    ocmp, idx = _sample_cmp(page_table, _kv_in(cache_cmp_kv[0]), one(q_s), cw, c2s_s, t_pos=past)
    idx = idx[:, :KV_HEADS, :TOP_N]
    win_t = _kv_in(state_win_kv[0]).reshape(dbsz, KV_ROW, w_len)
    o_s, win_s = _sample_attn(page_table, idx, _kv_in(cache_sel_kv[0]), one(q_s), one(gn_s), one(rs_s), rwt_s,
                              win_t, ocmp, t_pos=past)
    uin_s = uin_s.reshape(dbsz, D_CONV)
    xs = _merge(xs, norm_mix, uin_s, state_conv[0, :, 1], state_conv[0, :, 0], o_s.reshape(dbsz, D_ATTN), w_b,
                conv_w[0], w_co, w_ao, w_mo, seq_len=None)
    y_sample = _ffn(xs, *ffn2, g_final, final_norm=True).reshape(dbsz, 1, D_MODEL)
    new_out = lambda rt: rt.reshape(1, 2, KV_HEADS, HEAD_DIM, dbsz).transpose(0, 4, 1, 2, 3)[:, :, None]
    cmp_s = new_out(rct_s)
    sel_s = new_out(rst_s)
    win_s = _kv_out(win_s)
    conv_s = jnp.concatenate([state_conv[0, :, 1:], uin_s[:, None, :]], axis=1)[None]
    return (y_prompt, y_sample, cmp_p, sel_p, win_p, conv_p, cmp_s, sel_s, win_s, conv_s)
```

```python
import functools

import numpy as np
import jax
import jax.numpy as jnp
from jax import lax
from jax.experimental import pallas as pl
from jax.experimental.pallas import tpu as pltpu

F32 = jnp.float32
BF16 = jnp.bfloat16

D_MODEL = 1024
D_CONV = 512
CONV_W = 3
N_HEADS = 8
KV_HEADS = 2
HPG = N_HEADS // KV_HEADS
HEAD_DIM = 64
D_ATTN = N_HEADS * HEAD_DIM
KV_ROW = 2 * KV_HEADS * HEAD_DIM
CMP_STRIDE = 16
CMP_RATIO = 2
CMP_BLK = CMP_STRIDE * CMP_RATIO
CMP_HID = 128
SEL_BLK = 64
TOP_N = 16
WINDOW = 512
N_BRANCH = 3
D_FF = 2816
ROPE_THETA = 10000.0
EPS = 1e-6
NEG = -1e30
BIG = 1e9
GONE = -3e38
IN_SPLITS = (D_CONV, D_CONV, D_CONV, D_ATTN, KV_ROW, KV_ROW, KV_ROW, N_HEADS * N_BRANCH, 2 * D_MODEL)

LANES = 128
KV_LANES = KV_HEADS * HEAD_DIM
assert KV_LANES == LANES and KV_HEADS == 2 and 2 * HEAD_DIM == LANES
HEAD_ORDER = tuple(half * HPG + v for v in range(HPG) for half in range(KV_HEADS))
VMEM_LIMIT = 56 * 1024 * 1024
ROW_TILE = 512
KEY_TILE = 256


def _cparams(n_grid):
    return pltpu.CompilerParams(dimension_semantics=("arbitrary",) * n_grid, vmem_limit_bytes=VMEM_LIMIT)


def _const_spec(shape):
    nd = len(shape)
    return pl.BlockSpec(shape, lambda *_: (0,) * nd, pipeline_mode=pl.Buffered(1))


def _rms(x, g):
    return x * lax.rsqrt(jnp.mean(x * x, axis=-1, keepdims=True) + EPS) * g


def _silu(x):
    return x * jax.nn.sigmoid(x)


def _div(x, n):
    return lax.shift_right_logical(x, int(np.log2(n)))


def _mod(x, n):
    return x & (n - 1)


def _dot(a, b):
    return jnp.dot(a, b, preferred_element_type=F32)


def _dot_nt(a, b):
    return lax.dot_general(a, b, (((1,), (1,)), ((), ())), preferred_element_type=F32)


def _ffn_body(x_ref, g_ref, wg_ref, wu_ref, wd_ref, gf_ref, o_ref, *, final_norm):
    x = x_ref[...]
    h = _rms(x, g_ref[...]).astype(BF16)
    a = (_silu(_dot(h, wg_ref[...])) * _dot(h, wu_ref[...])).astype(BF16)
    y = x + 0.5 * _dot(a, wd_ref[...])
    if final_norm:
        y = _rms(y, gf_ref[...])
    o_ref[...] = y


def _ffn(x, g, wg, wu, wd, gf, *, final_norm):
    m = x.shape[0]
    tm = min(ROW_TILE, m)
    row = pl.BlockSpec((tm, D_MODEL), lambda i: (i, 0))
    return pl.pallas_call(
        functools.partial(_ffn_body, final_norm=final_norm),
        grid=(m // tm,),
        in_specs=[row, _const_spec((1, D_MODEL)), _const_spec(wg.shape), _const_spec(wu.shape),
                  _const_spec(wd.shape), _const_spec((1, D_MODEL))],
        out_specs=row,
        out_shape=jax.ShapeDtypeStruct((m, D_MODEL), F32),
        compiler_params=_cparams(1),
        name="ffn",
    )(x, g, wg, wu, wd, gf)


ROW_COLS = {"uc": (0, 512), "ux": (512, 1024), "q": (1024, 1536), "gn": (1536, 1664), "kvs": (1664, 1920),
            "kvw": (1920, 2176)}


def _rope_tile(z, cos, sin):
    first = _mod(lax.broadcasted_iota(jnp.int32, z.shape, 1), HEAD_DIM) < (HEAD_DIM // 2)
    rot = jnp.where(first, pltpu.roll(z, LANES - HEAD_DIM // 2, axis=1), pltpu.roll(z, HEAD_DIM // 2, axis=1))
    return z * cos + rot * sin


def _rope_rows_t(zt, cos_t, sin_t):
    half = HEAD_DIM // 2
    out = []
    for hh in range(KV_LANES // HEAD_DIM):
        x1 = zt[hh * HEAD_DIM:hh * HEAD_DIM + half]
        x2 = zt[hh * HEAD_DIM + half:(hh + 1) * HEAD_DIM]
        out += [x1 * cos_t - x2 * sin_t, x1 * sin_t + x2 * cos_t]
    return jnp.concatenate(out, axis=0)


def _mixin_body(x_ref, g_ref, wr_ref, wt_ref, cos_ref, sin_ref, cost_ref, sint_ref,
                uin_ref, q_ref, gn_ref, rct_ref, rst_ref, rwt_ref, kst_ref, kwt_ref, *row_refs):
    h = _rms(x_ref[0], g_ref[...]).astype(BF16)
    cos = cos_ref[...]
    sin = sin_ref[...]

    def proj(name):
        lo, hi = ROW_COLS[name]
        return _dot(h, wr_ref[:, lo:hi])

    uin_ref[0] = proj("uc") * proj("ux")
    zq = proj("q")
    scale = HEAD_DIM ** -0.5
    q_ref[0] = jnp.concatenate(
        [_rope_tile(zq[:, v * LANES:(v + 1) * LANES], cos, sin) * scale for v in range(D_ATTN // LANES)],
        axis=1).astype(BF16)
    gn_ref[0] = jax.nn.sigmoid(proj("gn"))
    cos_t = cost_ref[...]
    sin_t = sint_ref[...]
    for i, (f_ref, b_ref) in enumerate(((rct_ref, None), (rst_ref, kst_ref), (rwt_ref, kwt_ref))):
        zt = _dot_nt(wt_ref[i * KV_ROW:(i + 1) * KV_ROW, :], h)
        rows_t = jnp.concatenate([_rope_rows_t(zt[:KV_LANES], cos_t, sin_t), zt[KV_LANES:]], axis=0)
        f_ref[0] = rows_t
        if b_ref is not None:
            b_ref[0] = rows_t.astype(BF16)
    for name, r_ref in zip(("kvs", "kvw"), row_refs):
        z = proj(name)
        r_ref[0] = jnp.concatenate([_rope_tile(z[:, :KV_LANES], cos, sin), z[:, KV_LANES:]], axis=1)


def _mixin(x, g, wr, wt, tabs, *, rows_out):
    b, t, _ = x.shape
    tm = min(ROW_TILE, t)
    row = lambda n: pl.BlockSpec((1, tm, n), lambda i, j: (i, j, 0))
    col = pl.BlockSpec((1, KV_ROW, tm), lambda i, j: (i, 0, j))
    tab = pl.BlockSpec((tm, LANES), lambda i, j: (j, 0))
    tab_t = pl.BlockSpec((HEAD_DIM // 2, tm), lambda i, j: (0, j))
    rsd = lambda n, dt: jax.ShapeDtypeStruct((b, t, n), dt)
    csd = lambda dt: jax.ShapeDtypeStruct((b, KV_ROW, t), dt)
    extra = 2 if rows_out else 0
    return pl.pallas_call(
        _mixin_body,
        grid=(b, t // tm),
        in_specs=[row(D_MODEL), _const_spec((1, D_MODEL)), _const_spec(wr.shape), _const_spec(wt.shape),
                  tab, tab, tab_t, tab_t],
        out_specs=[row(D_CONV), row(D_ATTN), row(LANES), col, col, col, col, col] + [row(KV_ROW)] * extra,
        out_shape=[rsd(D_CONV, F32), rsd(D_ATTN, BF16), rsd(LANES, F32), csd(F32), csd(F32), csd(F32), csd(BF16),
                   csd(BF16)] + [rsd(KV_ROW, F32)] * extra,
        compiler_params=_cparams(2),
        name="mix_in",
    )(x, g, wr, wt, *tabs)


CHUNK_PITCH = 24


def _store_chunks(rows_ref, kv, first_chunk, rows):
    for j in range(rows.shape[0] // CMP_STRIDE):
        lo = (first_chunk + j) * CHUNK_PITCH
        rows_ref[kv, lo:lo + CMP_STRIDE, :] = rows[j * CMP_STRIDE:(j + 1) * CMP_STRIDE]


def _compress(rows_ref, n_chunks, pef_ref, w1_ref, w2k_ref, w2v_ref):
    low = lax.broadcasted_iota(jnp.int32, (n_chunks, LANES), 1) < HEAD_DIM
    keep = lax.broadcasted_iota(jnp.int32, (n_chunks, LANES), 0) < n_chunks - 1
    out = []
    for kv, w2_ref in enumerate((w2k_ref, w2v_ref)):
        lhs = [[], []]
        for s in range(0, CMP_STRIDE, 2):
            a0 = rows_ref[kv, pl.ds(s, n_chunks, stride=CHUNK_PITCH), :]
            a1 = rows_ref[kv, pl.ds(s + 1, n_chunks, stride=CHUNK_PITCH), :]
            lhs[0].append(jnp.where(low, a0, pltpu.roll(a1, HEAD_DIM, axis=1)).astype(BF16))
            lhs[1].append(jnp.where(low, pltpu.roll(a0, HEAD_DIM, axis=1), a1).astype(BF16))
        pe_part = _dot(pef_ref[kv], w1_ref[kv])
        bias = pe_part[0:1, :CMP_HID] + pe_part[1:2, CMP_HID:]
        hid = []
        for g in range(KV_HEADS):
            part = _dot(jnp.concatenate(lhs[g], axis=1), w1_ref[kv])
            hid.append(part[:, :CMP_HID] + pltpu.roll(part[:, CMP_HID:], n_chunks - 1, axis=0) + bias)
        act = _silu(jnp.concatenate(hid, axis=1)).astype(BF16)
        out.append(jnp.where(keep, _dot(act, w2_ref[...]), 0.0))
    return out


def _compress_prompt_body(xt_ref, pef_ref, w1_ref, w2k_ref, w2v_ref, kct_ref, vc_ref, rows_ref,
                          *, n_chunks, t_len):
    for kv in range(2):
        for c in range(t_len // LANES):
            _store_chunks(rows_ref, kv, c * (LANES // CMP_STRIDE),
                          xt_ref[0, kv * KV_LANES:(kv + 1) * KV_LANES, c * LANES:(c + 1) * LANES].T)
    kc, vc = _compress(rows_ref, n_chunks, pef_ref, w1_ref, w2k_ref, w2v_ref)
    kct_ref[0] = kc.T.astype(BF16)
    vc_ref[0] = vc.astype(BF16)


def _compress_prompt(rct, cw):
    b, _, t = rct.shape
    n_chunks = t // CMP_STRIDE
    out = pl.BlockSpec((1, n_chunks, LANES), lambda i: (i, 0, 0))
    return pl.pallas_call(
        functools.partial(_compress_prompt_body, n_chunks=n_chunks, t_len=t),
        grid=(b,),
        in_specs=[pl.BlockSpec((1, KV_ROW, t), lambda i: (i, 0, 0))] + [_const_spec(w.shape) for w in cw],
        out_specs=[out, out],
        out_shape=[jax.ShapeDtypeStruct((b, n_chunks, LANES), BF16)] * 2,
        scratch_shapes=[pltpu.VMEM((2, n_chunks * CHUNK_PITCH, LANES), F32)],
        compiler_params=_cparams(1),
        name="compress_prompt",
    )(rct, *cw)


def _softmax_rows(s):
    e = jnp.exp(s - jnp.max(s, axis=-1, keepdims=True))
    return e / jnp.sum(e, axis=-1, keepdims=True)


def _split_hi_lo(x):
    hi = x.astype(BF16)
    return hi, (x - hi.astype(F32)).astype(BF16)


def _nsa_prompt_body(q_ref, gn_ref, kct_ref, vc_ref, ks_ref, kw_ref, e_ref, c2st_ref, o_ref,
                     qaug_ref, p_ref, m_ref, l_ref, acc_ref, alpha_ref, out_ref, *, tq, kt, t_len):
    n_slots = N_HEADS
    rows = n_slots * tq
    n_sel = t_len // SEL_BLK
    start = pl.program_id(1) * tq
    lane = lax.broadcasted_iota(jnp.int32, (tq, LANES), 1)
    t_pos = start + lax.broadcasted_iota(jnp.int32, (tq, LANES), 0)
    low = lane < HEAD_DIM
    gn = gn_ref[0]
    slot_rows = [slice(slot * tq, (slot + 1) * tq) for slot in range(n_slots)]

    def gate(slot, branch):
        v, half = divmod(slot, KV_HEADS)
        c = (half * HPG + v) * N_BRANCH + branch
        return gn[:, c:c + 1]

    q = q_ref[0].astype(F32)
    for slot, rs in enumerate(slot_rows):
        v, half = divmod(slot, KV_HEADS)
        qv = q[:, v * LANES:(v + 1) * LANES]
        qaug_ref[rs, :LANES] = (jnp.where(low, qv, 0.0) if half == 0 else jnp.where(low, 0.0, qv)).astype(BF16)

    cmp_ok = lane * CMP_STRIDE + (CMP_BLK - 1) <= t_pos
    s_c = _dot(qaug_ref[:, :LANES], kct_ref[0])
    p_c = jnp.concatenate(
        [jnp.where(cmp_ok, _softmax_rows(jnp.where(cmp_ok, s_c[rs], NEG)), 0.0) for rs in slot_rows], axis=0)
    o_c = _dot(p_c.astype(BF16), vc_ref[0])
    for slot, rs in enumerate(slot_rows):
        out_ref[rs, :] = gate(slot, 0) * o_c[rs]

    blk = lax.broadcasted_iota(jnp.int32, (n_sel, tq), 0)
    cur = _div(start + lax.broadcasted_iota(jnp.int32, (n_sel, tq), 1), SEL_BLK)
    valid = blk <= cur
    forced = valid & ((blk == 0) | (blk == cur) | (blk == cur - 1))
    for half in range(KV_HEADS):
        psum = p_c[slot_rows[half]]
        for v in range(1, HPG):
            psum = psum + p_c[slot_rows[v * KV_HEADS + half]]
        hi, lo = _split_hi_lo(psum)
        imp = (_dot_nt(c2st_ref[...], hi) + _dot_nt(c2st_ref[...], lo))[:n_sel]
        score = jnp.where(forced, BIG, jnp.where(valid, imp, -BIG))
        rank = jnp.zeros((n_sel, tq), F32)
        for i in range(n_sel):
            r = score[i:i + 1, :]
            tie = jnp.where(blk > i, 1.0, 0.0)
            rank = rank + jnp.where(r > score, 1.0, jnp.where(r == score, tie, 0.0))
        pen_t = jnp.concatenate([jnp.where(rank >= TOP_N, NEG, 0.0), jnp.zeros((LANES - n_sel, tq), F32)], axis=0)
        pen = pen_t.T.astype(BF16)
        for v in range(HPG):
            qaug_ref[slot_rows[v * KV_HEADS + half], LANES:] = pen

    def reset():
        m_ref[...] = jnp.full((rows, LANES), NEG, F32)
        l_ref[...] = jnp.zeros((rows, LANES), F32)
        acc_ref[...] = jnp.zeros((rows, LANES), F32)

    d = lax.broadcasted_iota(jnp.int32, (tq, kt), 1) - lax.broadcasted_iota(jnp.int32, (tq, kt), 0)
    visible = {"causal": d <= 0, "lower": d > 0}

    def tile_step(kv_ref, koff, select, masks):
        w = len(masks) * kt
        kv = kv_ref[0, :, pl.ds(koff, w)]
        if select:
            s_all = _dot(qaug_ref[...], jnp.concatenate([kv[:KV_LANES], e_ref[:, pl.ds(koff, w)]], axis=0))
        else:
            s_all = _dot(qaug_ref[:, :LANES], kv[:KV_LANES])
        for rs in slot_rows:
            s = jnp.concatenate(
                [s_all[rs, jb * kt:(jb + 1) * kt] if mask is None
                 else jnp.where(visible[mask], s_all[rs, jb * kt:(jb + 1) * kt], NEG)
                 for jb, mask in enumerate(masks)], axis=1)
            m_prev = m_ref[rs, :]
            m_new = jnp.maximum(m_prev, jnp.max(s, axis=-1, keepdims=True))
            alpha_ref[rs, :] = jnp.exp(m_prev - m_new)
            p_ref[rs, :w] = jnp.exp(s - jnp.concatenate([m_new] * (w // LANES), axis=1)).astype(BF16)
            m_ref[rs, :] = m_new
        pv = _dot_nt(p_ref[:, :w], jnp.concatenate([kv[KV_LANES:], jnp.ones((KV_LANES, w), BF16)], axis=0))
        acc_ref[...] = alpha_ref[...] * acc_ref[...] + pv[:, :LANES]
        l_ref[...] = alpha_ref[...] * l_ref[...] + pv[:, LANES:]

    def finish(branch):
        for slot, rs in enumerate(slot_rows):
            out_ref[rs, :] = out_ref[rs, :] + gate(slot, branch) * (acc_ref[rs, :] / l_ref[rs, :])

    reset()
    n_full = start // kt

    def sel_body(j, carry):
        tile_step(ks_ref, pl.multiple_of(j * 2 * kt, 2 * kt), True, (None, None))
        return carry

    lax.fori_loop(0, n_full // 2, sel_body, 0)

    @pl.when(n_full % 2 == 1)
    def _():
        tile_step(ks_ref, pl.multiple_of(start - kt, kt), True, (None, "causal"))

    @pl.when(n_full % 2 == 0)
    def _():
        tile_step(ks_ref, pl.multiple_of(start, kt), True, ("causal",))
    finish(1)

    reset()
    n_win = WINDOW // kt
    for n_before in range(n_win + 1):
        masks = [None] * n_before + ["causal"]
        if n_before == n_win:
            masks[0] = "lower"

        @pl.when(start >= WINDOW if n_before == n_win else start == n_before * kt)
        def _():
            tile_step(kw_ref, pl.multiple_of(start - n_before * kt, kt), False, masks)
    finish(2)

    o_ref[0] = jnp.concatenate(
        [jnp.where(low, out_ref[slot_rows[2 * v], :], out_ref[slot_rows[2 * v + 1], :]) for v in range(HPG)],
        axis=1).astype(BF16)


def _nsa_prompt(q, gn, kct, vc, kst, kwt, e_t, c2s_t):
    b, t, _ = q.shape
    tq = kt = KEY_TILE
    assert WINDOW % kt == 0 and t % kt == 0
    n_cmp = vc.shape[1]
    rows = N_HEADS * tq
    tile = lambda n: pl.BlockSpec((1, tq, n), lambda i, j: (i, j, 0))
    per_b = lambda r, n: pl.BlockSpec((1, r, n), lambda i, j: (i, 0, 0))
    rows_f32 = pltpu.VMEM((rows, LANES), F32)
    return pl.pallas_call(
        functools.partial(_nsa_prompt_body, tq=tq, kt=kt, t_len=t),
        grid=(b, t // tq),
        in_specs=[tile(D_ATTN), tile(LANES), per_b(LANES, n_cmp), per_b(n_cmp, LANES), per_b(KV_ROW, t),
                  per_b(KV_ROW, t), _const_spec(e_t.shape), _const_spec(c2s_t.shape)],
        out_specs=tile(D_ATTN),
        out_shape=jax.ShapeDtypeStruct((b, t, D_ATTN), BF16),
        scratch_shapes=[pltpu.VMEM((rows, 2 * LANES), BF16), pltpu.VMEM((rows, max(2 * kt, WINDOW + kt)), BF16), rows_f32, rows_f32,
                        rows_f32, rows_f32, rows_f32],
        compiler_params=_cparams(2),
        name="nsa_prompt",
    )(q, gn, kct, vc, kst, kwt, e_t, c2s_t)


def _merge_body(x_ref, g_ref, u_ref, um1_ref, um2_ref, o_ref, wb_ref, cw_ref, wco_ref, wao_ref, wmo_ref,
                y_ref, *, tm, tiles_per_seq):
    x = x_ref[...]
    h = _rms(x, g_ref[...]).astype(BF16)
    ub = _dot(h, wb_ref[:, :D_CONV])
    u = u_ref[...]
    if tiles_per_seq is None:
        um1 = um1_ref[...]
        um2 = um2_ref[...]
    else:
        row = lax.broadcasted_iota(jnp.int32, (tm, D_CONV), 0)
        keep = jnp.where(pl.program_id(0) % tiles_per_seq == 0, 0.0, 1.0)
        prev = um1_ref[...] * keep
        um1 = jnp.where(row == 0, prev[7:8, :], pltpu.roll(u, 1, axis=0))
        um2 = jnp.where(row == 0, prev[6:7, :], jnp.where(row == 1, prev[7:8, :], pltpu.roll(u, 2, axis=0)))
    conv = cw_ref[0:1, :] * um2 + cw_ref[1:2, :] * um1 + cw_ref[2:3, :] * u
    y_c = _dot((ub * conv).astype(BF16), wco_ref[...])
    y_a = _dot(o_ref[...], wao_ref[...])
    g_c = jax.nn.sigmoid(_dot(h, wb_ref[:, D_CONV:D_CONV + D_MODEL]))
    g_a = jax.nn.sigmoid(_dot(h, wb_ref[:, D_CONV + D_MODEL:]))
    y_ref[...] = x + _dot((g_c * y_c + g_a * y_a).astype(BF16), wmo_ref[...])


def _merge(x, g, uin, um1, um2, o, wb, cw, wco, wao, wmo, *, seq_len):
    m = x.shape[0]
    tm = min(ROW_TILE, m)
    row = lambda n: pl.BlockSpec((tm, n), lambda i: (i, 0))
    if seq_len is None:
        prev_specs = [row(D_CONV), row(D_CONV)]
        tiles_per_seq = None
    else:
        halo = pl.BlockSpec((8, D_CONV), lambda i: (jnp.maximum(i * (tm // 8) - 1, 0), 0))
        prev_specs = [halo, halo]
        tiles_per_seq = seq_len // tm
    return pl.pallas_call(
        functools.partial(_merge_body, tm=tm, tiles_per_seq=tiles_per_seq),
        grid=(m // tm,),
        in_specs=[row(D_MODEL), _const_spec((1, D_MODEL)), row(D_CONV)] + prev_specs +
                 [row(D_ATTN), _const_spec(wb.shape), _const_spec(cw.shape), _const_spec(wco.shape),
                  _const_spec(wao.shape), _const_spec(wmo.shape)],
        out_specs=row(D_MODEL),
        out_shape=jax.ShapeDtypeStruct((m, D_MODEL), F32),
        compiler_params=_cparams(1),
        name="merge",
    )(x, g, uin, um1, um2, o, wb, cw, wco, wao, wmo)


def _expand_q(q_ref):
    row = lax.broadcasted_iota(jnp.int32, (N_HEADS, LANES), 0)
    lane = lax.broadcasted_iota(jnp.int32, (N_HEADS, LANES), 1)
    q = q_ref[0].astype(F32)
    qe = jnp.zeros((N_HEADS, LANES), F32)
    for v in range(HPG):
        qv = jnp.broadcast_to(q[:, v * LANES:(v + 1) * LANES], (N_HEADS, LANES))
        qe = jnp.where((_div(row, KV_HEADS) == v) & (_div(lane, HEAD_DIM) == _mod(row, KV_HEADS)), qv, qe)
    return qe


def _sample_cmp_body(pt_ref, cache_ref, q_ref, pef_ref, w1_ref, w2k_ref, w2v_ref, c2s_ref,
                     ocmp_ref, idx_ref, buf_ref, rows_ref, sem_ref, *, n_pages, page, t_pos):
    b = pl.program_id(0)
    n_b = pl.num_programs(0)
    n_chunks = n_pages * page // CMP_STRIDE
    n_sel = t_pos // SEL_BLK + 1

    def page_copy(bb, p, slot):
        return pltpu.make_async_copy(cache_ref.at[pt_ref[bb * n_pages + p]], buf_ref.at[slot, p], sem_ref.at[slot])

    def start_all(bb, slot):
        for p in range(n_pages):
            page_copy(bb, p, slot).start()

    @pl.when(b == 0)
    def _():
        start_all(0, 0)

    @pl.when(b + 1 < n_b)
    def _():
        start_all(b + 1, (b + 1) % 2)

    slot = b % 2
    for p in range(n_pages):
        page_copy(b, p, slot).wait()

    group = 8
    for p0 in range(0, n_pages, group):
        for kv in range(2):
            wide = jnp.concatenate([buf_ref[slot, p0 + i, kv] for i in range(group)], axis=1)
            _store_chunks(rows_ref, kv, p0 * page // CMP_STRIDE, wide.T)
    kc, vc = _compress(rows_ref, n_chunks, pef_ref, w1_ref, w2k_ref, w2v_ref)

    lane = lax.broadcasted_iota(jnp.int32, (N_HEADS, LANES), 1)
    qe = _expand_q(q_ref)
    cidx = lax.broadcasted_iota(jnp.int32, (N_HEADS, n_chunks), 1)
    cmp_ok = cidx * CMP_STRIDE + (CMP_BLK - 1) <= t_pos
    p = jnp.where(cmp_ok, _softmax_rows(jnp.where(cmp_ok, _dot_nt(qe.astype(BF16), kc.astype(BF16)), NEG)), 0.0)
    ocmp_ref[0] = _dot(p.astype(BF16), vc.astype(BF16))

    hi, lo = _split_hi_lo(p)
    imp_h = _dot(hi, c2s_ref[...]) + _dot(lo, c2s_ref[...])
    n_lanes = c2s_ref.shape[1]
    grow = lax.broadcasted_iota(jnp.int32, (N_HEADS, n_lanes), 0)
    glane = lax.broadcasted_iota(jnp.int32, (N_HEADS, n_lanes), 1)
    imp = jnp.zeros((N_HEADS, n_lanes), F32)
    for g in range(KV_HEADS):
        tot = jnp.sum(jnp.where(_mod(grow, KV_HEADS) == g, imp_h, 0.0), axis=0, keepdims=True)
        imp = jnp.where(grow == g, jnp.broadcast_to(tot, imp.shape), imp)
    cur = t_pos // SEL_BLK
    forced = (glane == 0) | (glane == cur) | (glane == cur - 1)
    score = jnp.where(glane < n_sel, jnp.where(forced, BIG, imp), GONE)
    flane = glane.astype(F32)
    picked = jnp.zeros((N_HEADS, LANES), jnp.int32)
    for k in range(TOP_N):
        best = jnp.max(score, axis=-1, keepdims=True)
        where_best = jnp.min(jnp.where(score == best, flane, float(n_lanes)), axis=-1, keepdims=True)
        picked = jnp.where(lane == k, where_best.astype(jnp.int32), picked)
        score = jnp.where(flane == where_best, GONE, score)
    idx_ref[0] = picked


def _sample_cmp(page_table, cache, q, cw, c2s, *, t_pos):
    bsz, n_pages = page_table.shape
    page = cache.shape[-1]
    assert page == LANES
    gs = pltpu.PrefetchScalarGridSpec(
        num_scalar_prefetch=1,
        grid=(bsz,),
        in_specs=[pl.BlockSpec(memory_space=pl.ANY), pl.BlockSpec((1, 1, D_ATTN), lambda i, pt: (i, 0, 0))] +
                 [pl.BlockSpec(w.shape, (lambda nd: lambda i, pt: (0,) * nd)(w.ndim), pipeline_mode=pl.Buffered(1))
                  for w in (*cw, c2s)],
        out_specs=[pl.BlockSpec((1, N_HEADS, LANES), lambda i, pt: (i, 0, 0))] * 2,
        scratch_shapes=[pltpu.VMEM((2, n_pages, 2, KV_LANES, page), F32),
                        pltpu.VMEM((2, n_pages * page // CMP_STRIDE * CHUNK_PITCH, KV_LANES), F32),
                        pltpu.SemaphoreType.DMA((2,))],
    )
    return pl.pallas_call(
        functools.partial(_sample_cmp_body, n_pages=n_pages, page=page, t_pos=t_pos),
        grid_spec=gs,
        out_shape=[jax.ShapeDtypeStruct((bsz, N_HEADS, LANES), F32),
                   jax.ShapeDtypeStruct((bsz, N_HEADS, LANES), jnp.int32)],
        compiler_params=_cparams(1),
        name="sample_cmp",
    )(page_table.reshape(-1), cache, q, *cw, c2s)


def _sample_attn_body(pt_ref, idx_ref, cache_ref, q_ref, gn_ref, ns_ref, nwt_ref, win_ref, ocmp_ref,
                      o_ref, wout_ref, buf_ref, sem_ref, *, n_pages, page, t_pos):
    b = pl.program_id(0)
    n_b = pl.num_programs(0)
    halves = page // SEL_BLK
    new_blk = t_pos // SEL_BLK
    n_slot = KV_HEADS * TOP_N

    def blk_of(bb, k):
        return idx_ref[bb * n_slot + k]

    def page_copy(bb, k, slot):
        j = jnp.minimum(blk_of(bb, k), new_blk - 1)
        return pltpu.make_async_copy(cache_ref.at[pt_ref[bb * n_pages + j // halves]], buf_ref.at[slot, k],
                                     sem_ref.at[slot])

    def start_all(bb, slot):
        for k in range(n_slot):
            page_copy(bb, k, slot).start()

    @pl.when(b == 0)
    def _():
        start_all(0, 0)

    @pl.when(b + 1 < n_b)
    def _():
        start_all(b + 1, (b + 1) % 2)

    slot = b % 2
    for k in range(n_slot):
        page_copy(b, k, slot).wait()

    row = lax.broadcasted_iota(jnp.int32, (N_HEADS, LANES), 0)
    lane = lax.broadcasted_iota(jnp.int32, (N_HEADS, LANES), 1)
    grp0 = _mod(row, KV_HEADS) == 0
    qe = _expand_q(q_ref)
    qb = qe.astype(BF16)

    n_keys = TOP_N * page
    klane = lax.broadcasted_iota(jnp.int32, (N_HEADS, n_keys), 1)
    s_g = []
    for g in range(KV_HEADS):
        kt_g = jnp.concatenate([buf_ref[slot, g * TOP_N + k, 0] for k in range(TOP_N)], axis=1).astype(BF16)
        pen = jnp.full((N_HEADS, n_keys), NEG, F32)
        for k in range(TOP_N):
            j = blk_of(b, g * TOP_N + k)
            lo = k * page + (j % halves) * SEL_BLK
            seen = jnp.where(j >= new_blk, NEG, 0.0)
            pen = jnp.where((klane >= lo) & (klane < lo + SEL_BLK), seen, pen)
        s_g.append(_dot(qb, kt_g) + pen)
    grp0_k = _mod(lax.broadcasted_iota(jnp.int32, (N_HEADS, n_keys), 0), KV_HEADS) == 0
    s = jnp.where(grp0_k, s_g[0], s_g[1])
    r_new = ns_ref[0].astype(BF16).astype(F32)
    k_new, v_new = r_new[:, :KV_LANES], r_new[:, KV_LANES:]
    s_new = jnp.sum(qe * k_new, axis=-1, keepdims=True)
    m = jnp.maximum(jnp.max(s, axis=-1, keepdims=True), s_new)
    p = jnp.exp(s - m)
    p_new = jnp.exp(s_new - m)
    den = jnp.sum(p, axis=-1, keepdims=True) + p_new
    pb = p.astype(BF16)
    pv = []
    for g in range(KV_HEADS):
        vt_g = jnp.concatenate([buf_ref[slot, g * TOP_N + k, 1] for k in range(TOP_N)], axis=1).astype(BF16)
        pv.append(_dot_nt(pb, vt_g))
    o_sel = (jnp.where(grp0, pv[0], pv[1]) + p_new.astype(BF16).astype(F32) * v_new) / den

    w_len = win_ref.shape[2]
    nw = nwt_ref[0]
    pick = lax.broadcasted_iota(jnp.int32, nw.shape, 1) == b
    new_col = jnp.sum(jnp.where(pick, nw, 0.0), axis=-1, keepdims=True)
    wlane = lax.broadcasted_iota(jnp.int32, (KV_ROW, w_len), 1)
    wout = jnp.where(wlane == w_len - 1, new_col, pltpu.roll(win_ref[0], w_len - 1, axis=1))
    wout_ref[0] = wout
    sw = _dot(qb, wout[:KV_LANES].astype(BF16))
    ew = jnp.exp(sw - jnp.max(sw, axis=-1, keepdims=True))
    o_win = _dot_nt(ew.astype(BF16), wout[KV_LANES:].astype(BF16)) / jnp.sum(ew, axis=-1, keepdims=True)

    gn = jnp.broadcast_to(gn_ref[0], (N_HEADS, LANES))
    head = _mod(row, KV_HEADS) * HPG + _div(row, KV_HEADS)

    def gate(branch):
        return jnp.sum(jnp.where(lane == head * N_BRANCH + branch, gn, 0.0), axis=-1, keepdims=True)

    out = gate(0) * ocmp_ref[0] + gate(1) * o_sel + gate(2) * o_win
    o_ref[0] = jnp.concatenate(
        [jnp.sum(jnp.where((_div(row, KV_HEADS) == v) & (_div(lane, HEAD_DIM) == _mod(row, KV_HEADS)), out, 0.0),
                 axis=0, keepdims=True)
         for v in range(HPG)], axis=1).astype(BF16)


def _sample_attn(page_table, idx, cache, q, gn, new_sel, new_win_t, win_t, ocmp, *, t_pos):
    bsz, n_pages = page_table.shape
    page = cache.shape[-1]
    w_len = win_t.shape[2]
    one = lambda n: pl.BlockSpec((1, 1, n), lambda i, *_: (i, 0, 0))
    win_spec = pl.BlockSpec((1, KV_ROW, w_len), lambda i, *_: (i, 0, 0))
    gs = pltpu.PrefetchScalarGridSpec(
        num_scalar_prefetch=2,
        grid=(bsz,),
        in_specs=[pl.BlockSpec(memory_space=pl.ANY), one(D_ATTN), one(LANES), one(KV_ROW),
                  pl.BlockSpec(new_win_t.shape, lambda i, *_: (0, 0, 0), pipeline_mode=pl.Buffered(1)), win_spec,
                  pl.BlockSpec((1, N_HEADS, LANES), lambda i, *_: (i, 0, 0))],
        out_specs=[one(D_ATTN), win_spec],
        scratch_shapes=[pltpu.VMEM((2, KV_HEADS * TOP_N, 2, KV_LANES, page), F32), pltpu.SemaphoreType.DMA((2,))],
    )
    return pl.pallas_call(
        functools.partial(_sample_attn_body, n_pages=n_pages, page=page, t_pos=t_pos),
        grid_spec=gs,
        out_shape=[jax.ShapeDtypeStruct((bsz, 1, D_ATTN), BF16), jax.ShapeDtypeStruct(win_t.shape, F32)],
        compiler_params=_cparams(1),
        name="sample_attn",
    )(page_table.reshape(-1), idx.reshape(-1), cache, q, gn, new_sel, new_win_t, win_t, ocmp)


def _rope_tables(pos):
    half = HEAD_DIM // 2
    freqs = jnp.power(ROPE_THETA, -jnp.arange(half, dtype=F32) * (2.0 / HEAD_DIM))
    ang = pos.astype(F32)[:, None] * freqs[None, :]
    cos, sin = jnp.cos(ang), jnp.sin(ang)
    reps = LANES // HEAD_DIM
    return (jnp.tile(jnp.concatenate([cos, cos], axis=1), (1, reps)),
            jnp.tile(jnp.concatenate([-sin, sin], axis=1), (1, reps)), cos.T, sin.T)


def _compress_weights(w1k, w2k, pek, w1v, w2v, pev):
    flat = CMP_STRIDE * HEAD_DIM

    def pe_rows(pe):
        return jnp.pad(pe.reshape(CMP_RATIO, flat), ((0, 8 - CMP_RATIO), (0, 0)))

    def first(w1):
        return w1.reshape(CMP_RATIO, flat, CMP_HID).transpose(1, 0, 2).reshape(flat, CMP_RATIO * CMP_HID)

    def pair(w2):
        z = jnp.zeros_like(w2)
        return jnp.concatenate([jnp.concatenate([w2, z], axis=1), jnp.concatenate([z, w2], axis=1)],
                               axis=0).astype(BF16)

    return (jnp.stack([pe_rows(pek), pe_rows(pev)]).astype(BF16), jnp.stack([first(w1k), first(w1v)]).astype(BF16),
            pair(w2k), pair(w2v))


def _cmp_to_sel(n_cmp_pad, n_sel_pad, n_cmp, n_sel):
    i = np.arange(n_cmp_pad)[:, None]
    j = np.arange(n_sel_pad)[None, :]
    start = i * CMP_STRIDE
    return (start < j * SEL_BLK + SEL_BLK) & (start + CMP_BLK > j * SEL_BLK) & (i < n_cmp) & (j < n_sel)


def _kv_out(rows_t):
    n, _, t = rows_t.shape
    return rows_t.reshape(1, n, 2, KV_HEADS, HEAD_DIM, t).transpose(0, 1, 5, 2, 3, 4)


def _kv_in(a):
    n, t = a.shape[:2]
    return a.transpose(0, 2, 3, 4, 1).reshape(n, 2, KV_LANES, t)


def kernel(x_prompt, x_sample, cache_cmp_kv, cache_sel_kv, state_win_kv, state_conv, page_table, norm_ffn1, ffn1_w_gate, ffn1_w_up, ffn1_w_down, norm_mix, w_mix_in, conv_w, w_cmpk1, w_cmpk2, pe_cmpk, w_cmpv1, w_cmpv2, pe_cmpv, w_conv_out, w_attn_out, w_mix_out, norm_ffn2, ffn2_w_gate, ffn2_w_up, ffn2_w_down, norm_final):
    depth = norm_ffn1.shape[0]
    assert depth == 1
    bsz, t_len, _ = x_prompt.shape
    dbsz, dseq, _ = x_sample.shape
    n_pages = page_table.shape[1]
    page = cache_cmp_kv.shape[2]
    past = n_pages * page
    w_len = state_win_kv.shape[2]
    assert dseq == 1 and w_len == WINDOW and past >= WINDOW and past % SEL_BLK == 0 and page % SEL_BLK == 0
    assert t_len >= WINDOW and t_len // CMP_STRIDE == LANES

    offs = np.cumsum(IN_SPLITS)[:-1].tolist()
    w_ub, w_uc, w_ux, w_q, w_kvc, w_kvs, w_kvw, w_gn, w_gb = jnp.split(w_mix_in[0], offs, axis=1)
    order = np.asarray(HEAD_ORDER)
    w_q = w_q.reshape(D_MODEL, N_HEADS, HEAD_DIM)[:, order].reshape(D_MODEL, D_ATTN)
    w_gn = jnp.pad(w_gn, ((0, 0), (0, LANES - w_gn.shape[1])))
    w_r = jnp.concatenate([w_uc, w_ux, w_q, w_gn], axis=1).astype(BF16)
    w_r_s = jnp.concatenate([w_uc, w_ux, w_q, w_gn, w_kvs, w_kvw], axis=1).astype(BF16)
    w_t = jnp.concatenate([w_kvc, w_kvs, w_kvw], axis=1).T.astype(BF16)
    w_b = jnp.concatenate([w_ub, w_gb], axis=1).astype(BF16)
    w_ao = w_attn_out[0].reshape(N_HEADS, HEAD_DIM, D_MODEL)[order].reshape(D_ATTN, D_MODEL).astype(BF16)
    w_co = w_conv_out[0].astype(BF16)
    w_mo = w_mix_out[0].astype(BF16)
    cw = _compress_weights(w_cmpk1[0], w_cmpk2[0], pe_cmpk[0], w_cmpv1[0], w_cmpv2[0], pe_cmpv[0])
    ffn1 = (norm_ffn1, ffn1_w_gate[0].astype(BF16), ffn1_w_up[0].astype(BF16), ffn1_w_down[0].astype(BF16))
    ffn2 = (norm_ffn2, ffn2_w_gate[0].astype(BF16), ffn2_w_up[0].astype(BF16), ffn2_w_down[0].astype(BF16))
    g_final = norm_final.reshape(1, D_MODEL)

    xp = _ffn(x_prompt.reshape(bsz * t_len, D_MODEL), *ffn1, g_final, final_norm=False)
    tabs = _rope_tables(jnp.arange(t_len, dtype=jnp.int32))
    uin, q, gn, rct, rst, rwt, kst, kwt = _mixin(xp.reshape(bsz, t_len, D_MODEL), norm_mix, w_r, w_t, tabs,
                                                 rows_out=False)
    kct, vc = _compress_prompt(rct, cw)
    n_chunks = t_len // CMP_STRIDE
    n_sel = t_len // SEL_BLK
    e_t = jnp.asarray(np.arange(LANES)[:, None] == np.arange(t_len)[None, :] // SEL_BLK, BF16)
    c2s_t = jnp.asarray(_cmp_to_sel(n_chunks, LANES, n_chunks - CMP_RATIO + 1, n_sel).T, BF16)
    o = _nsa_prompt(q, gn, kct, vc, kst, kwt, e_t, c2s_t)
    uin = uin.reshape(bsz * t_len, D_CONV)
    xp = _merge(xp, norm_mix, uin, uin, uin, o.reshape(bsz * t_len, D_ATTN), w_b, conv_w[0], w_co, w_ao, w_mo,
                seq_len=t_len)
    y_prompt = _ffn(xp, *ffn2, g_final, final_norm=True).reshape(bsz, t_len, D_MODEL)
    cmp_p = _kv_out(rct)
    sel_p = _kv_out(rst)
    win_p = _kv_out(rwt[:, :, t_len - WINDOW:])
    conv_p = uin.reshape(1, bsz, t_len, D_CONV)[:, :, t_len - (CONV_W - 1):]

    xs = _ffn(x_sample.reshape(dbsz, D_MODEL), *ffn1, g_final, final_norm=False)
    tabs_s = _rope_tables(jnp.full((dbsz,), past, jnp.int32))
    uin_s, q_s, gn_s, rct_s, rst_s, rwt_s, _, _, rs_s, _ = _mixin(xs.reshape(1, dbsz, D_MODEL), norm_mix, w_r_s,
                                                                  w_t, tabs_s, rows_out=True)
    one = lambda a: a.reshape(dbsz, 1, a.shape[-1])
    n_chunks_s = past // CMP_STRIDE
    n_sel_s = past // SEL_BLK + 1
    n_sel_pad = -(-n_sel_s // LANES) * LANES
    c2s_s = jnp.asarray(_cmp_to_sel(n_chunks_s, n_sel_pad, (past + 1) // CMP_STRIDE - CMP_RATIO + 1, n_sel_s), BF16)
    ocmp, idx = _sample_cmp(page_table, _kv_in(cache_cmp_kv[0]), one(q_s), cw, c2s_s, t_pos=past)
    idx = idx[:, :KV_HEADS, :TOP_N]
    win_t = _kv_in(state_win_kv[0]).reshape(dbsz, KV_ROW, w_len)
    o_s, win_s = _sample_attn(page_table, idx, _kv_in(cache_sel_kv[0]), one(q_s), one(gn_s), one(rs_s), rwt_s,
                              win_t, ocmp, t_pos=past)
    uin_s = uin_s.reshape(dbsz, D_CONV)
    xs = _merge(xs, norm_mix, uin_s, state_conv[0, :, 1], state_conv[0, :, 0], o_s.reshape(dbsz, D_ATTN), w_b,
                conv_w[0], w_co, w_ao, w_mo, seq_len=None)
    y_sample = _ffn(xs, *ffn2, g_final, final_norm=True).reshape(dbsz, 1, D_MODEL)
    new_out = lambda rt: rt.reshape(1, 2, KV_HEADS, HEAD_DIM, dbsz).transpose(0, 4, 1, 2, 3)[:, :, None]
    cmp_s = new_out(rct_s)
    sel_s = new_out(rst_s)
    win_s = _kv_out(win_s)
    conv_s = jnp.concatenate([state_conv[0, :, 1:], uin_s[:, None, :]], axis=1)[None]
    return (y_prompt, y_sample, cmp_p, sel_p, win_p, conv_p, cmp_s, sel_s, win_s, conv_s)
```

```python
import functools

import numpy as np
import jax
import jax.numpy as jnp
from jax import lax
from jax.experimental import pallas as pl
from jax.experimental.pallas import tpu as pltpu

F32 = jnp.float32
BF16 = jnp.bfloat16

D_MODEL = 1024
D_CONV = 512
CONV_W = 3
N_HEADS = 8
KV_HEADS = 2
HPG = N_HEADS // KV_HEADS
HEAD_DIM = 64
D_ATTN = N_HEADS * HEAD_DIM
KV_ROW = 2 * KV_HEADS * HEAD_DIM
CMP_STRIDE = 16
CMP_RATIO = 2
CMP_BLK = CMP_STRIDE * CMP_RATIO
CMP_HID = 128
SEL_BLK = 64
TOP_N = 16
WINDOW = 512
N_BRANCH = 3
D_FF = 2816
ROPE_THETA = 10000.0
EPS = 1e-6
NEG = -1e30
BIG = 1e9
GONE = -3e38
IN_SPLITS = (D_CONV, D_CONV, D_CONV, D_ATTN, KV_ROW, KV_ROW, KV_ROW, N_HEADS * N_BRANCH, 2 * D_MODEL)

LANES = 128
KV_LANES = KV_HEADS * HEAD_DIM
assert KV_LANES == LANES and KV_HEADS == 2 and 2 * HEAD_DIM == LANES
HEAD_ORDER = tuple(half * HPG + v for v in range(HPG) for half in range(KV_HEADS))
VMEM_LIMIT = 56 * 1024 * 1024
ROW_TILE = 512
KEY_TILE = 256


def _cparams(n_grid):
    return pltpu.CompilerParams(dimension_semantics=("arbitrary",) * n_grid, vmem_limit_bytes=VMEM_LIMIT)


def _const_spec(shape):
    nd = len(shape)
    return pl.BlockSpec(shape, lambda *_: (0,) * nd, pipeline_mode=pl.Buffered(1))


def _rms(x, g):
    return x * lax.rsqrt(jnp.mean(x * x, axis=-1, keepdims=True) + EPS) * g


def _silu(x):
    return x * jax.nn.sigmoid(x)


def _div(x, n):
    return lax.shift_right_logical(x, int(np.log2(n)))


def _mod(x, n):
    return x & (n - 1)


def _dot(a, b):
    return jnp.dot(a, b, preferred_element_type=F32)


def _dot_nt(a, b):
    return lax.dot_general(a, b, (((1,), (1,)), ((), ())), preferred_element_type=F32)


def _ffn_body(x_ref, g_ref, wg_ref, wu_ref, wd_ref, gf_ref, o_ref, *, final_norm):
    x = x_ref[...]
    h = _rms(x, g_ref[...]).astype(BF16)
    a = (_silu(_dot(h, wg_ref[...])) * _dot(h, wu_ref[...])).astype(BF16)
    y = x + 0.5 * _dot(a, wd_ref[...])
    if final_norm:
        y = _rms(y, gf_ref[...])
    o_ref[...] = y


def _ffn(x, g, wg, wu, wd, gf, *, final_norm):
    m = x.shape[0]
    tm = min(ROW_TILE, m)
    row = pl.BlockSpec((tm, D_MODEL), lambda i: (i, 0))
    return pl.pallas_call(
        functools.partial(_ffn_body, final_norm=final_norm),
        grid=(m // tm,),
        in_specs=[row, _const_spec((1, D_MODEL)), _const_spec(wg.shape), _const_spec(wu.shape),
                  _const_spec(wd.shape), _const_spec((1, D_MODEL))],
        out_specs=row,
        out_shape=jax.ShapeDtypeStruct((m, D_MODEL), F32),
        compiler_params=_cparams(1),
        name="ffn",
    )(x, g, wg, wu, wd, gf)


ROW_COLS = {"uc": (0, 512), "ux": (512, 1024), "q": (1024, 1536), "gn": (1536, 1664), "kvs": (1664, 1920),
            "kvw": (1920, 2176)}


def _rope_tile(z, cos, sin):
    first = _mod(lax.broadcasted_iota(jnp.int32, z.shape, 1), HEAD_DIM) < (HEAD_DIM // 2)
    rot = jnp.where(first, pltpu.roll(z, LANES - HEAD_DIM // 2, axis=1), pltpu.roll(z, HEAD_DIM // 2, axis=1))
    return z * cos + rot * sin


def _rope_rows_t(zt, cos_t, sin_t):
    half = HEAD_DIM // 2
    out = []
    for hh in range(KV_LANES // HEAD_DIM):
        x1 = zt[hh * HEAD_DIM:hh * HEAD_DIM + half]
        x2 = zt[hh * HEAD_DIM + half:(hh + 1) * HEAD_DIM]
        out += [x1 * cos_t - x2 * sin_t, x1 * sin_t + x2 * cos_t]
    return jnp.concatenate(out, axis=0)


def _mixin_body(x_ref, g_ref, wr_ref, wt_ref, cos_ref, sin_ref, cost_ref, sint_ref,
                uin_ref, q_ref, gn_ref, rct_ref, rst_ref, rwt_ref, kst_ref, kwt_ref, *row_refs):
    h = _rms(x_ref[0], g_ref[...]).astype(BF16)
    cos = cos_ref[...]
    sin = sin_ref[...]

    def proj(name):
        lo, hi = ROW_COLS[name]
        return _dot(h, wr_ref[:, lo:hi])

    uin_ref[0] = proj("uc") * proj("ux")
    zq = proj("q")
    scale = HEAD_DIM ** -0.5
    q_ref[0] = jnp.concatenate(
        [_rope_tile(zq[:, v * LANES:(v + 1) * LANES], cos, sin) * scale for v in range(D_ATTN // LANES)],
        axis=1).astype(BF16)
    gn_ref[0] = jax.nn.sigmoid(proj("gn"))
    cos_t = cost_ref[...]
    sin_t = sint_ref[...]
    for i, (f_ref, b_ref) in enumerate(((rct_ref, None), (rst_ref, kst_ref), (rwt_ref, kwt_ref))):
        zt = _dot_nt(wt_ref[i * KV_ROW:(i + 1) * KV_ROW, :], h)
        rows_t = jnp.concatenate([_rope_rows_t(zt[:KV_LANES], cos_t, sin_t), zt[KV_LANES:]], axis=0)
        f_ref[0] = rows_t
        if b_ref is not None:
            b_ref[0] = rows_t.astype(BF16)
    for name, r_ref in zip(("kvs", "kvw"), row_refs):
        z = proj(name)
        r_ref[0] = jnp.concatenate([_rope_tile(z[:, :KV_LANES], cos, sin), z[:, KV_LANES:]], axis=1)


def _mixin(x, g, wr, wt, tabs, *, rows_out):
    b, t, _ = x.shape
    tm = min(ROW_TILE, t)
    row = lambda n: pl.BlockSpec((1, tm, n), lambda i, j: (i, j, 0))
    col = pl.BlockSpec((1, KV_ROW, tm), lambda i, j: (i, 0, j))
    tab = pl.BlockSpec((tm, LANES), lambda i, j: (j, 0))
    tab_t = pl.BlockSpec((HEAD_DIM // 2, tm), lambda i, j: (0, j))
    rsd = lambda n, dt: jax.ShapeDtypeStruct((b, t, n), dt)
    csd = lambda dt: jax.ShapeDtypeStruct((b, KV_ROW, t), dt)
    extra = 2 if rows_out else 0
    return pl.pallas_call(
        _mixin_body,
        grid=(b, t // tm),
        in_specs=[row(D_MODEL), _const_spec((1, D_MODEL)), _const_spec(wr.shape), _const_spec(wt.shape),
                  tab, tab, tab_t, tab_t],
        out_specs=[row(D_CONV), row(D_ATTN), row(LANES), col, col, col, col, col] + [row(KV_ROW)] * extra,
        out_shape=[rsd(D_CONV, F32), rsd(D_ATTN, BF16), rsd(LANES, F32), csd(F32), csd(F32), csd(F32), csd(BF16),
                   csd(BF16)] + [rsd(KV_ROW, F32)] * extra,
        compiler_params=_cparams(2),
        name="mix_in",
    )(x, g, wr, wt, *tabs)


CHUNK_PITCH = 24


def _store_chunks(rows_ref, kv, first_chunk, rows):
    for j in range(rows.shape[0] // CMP_STRIDE):
        lo = (first_chunk + j) * CHUNK_PITCH
        rows_ref[kv, lo:lo + CMP_STRIDE, :] = rows[j * CMP_STRIDE:(j + 1) * CMP_STRIDE]


def _compress(rows_ref, n_chunks, pef_ref, w1_ref, w2k_ref, w2v_ref):
    low = lax.broadcasted_iota(jnp.int32, (n_chunks, LANES), 1) < HEAD_DIM
    keep = lax.broadcasted_iota(jnp.int32, (n_chunks, LANES), 0) < n_chunks - 1
    out = []
    for kv, w2_ref in enumerate((w2k_ref, w2v_ref)):
        lhs = [[], []]
        for s in range(0, CMP_STRIDE, 2):
            a0 = rows_ref[kv, pl.ds(s, n_chunks, stride=CHUNK_PITCH), :]
            a1 = rows_ref[kv, pl.ds(s + 1, n_chunks, stride=CHUNK_PITCH), :]
            lhs[0].append(jnp.where(low, a0, pltpu.roll(a1, HEAD_DIM, axis=1)).astype(BF16))
            lhs[1].append(jnp.where(low, pltpu.roll(a0, HEAD_DIM, axis=1), a1).astype(BF16))
        pe_part = _dot(pef_ref[kv], w1_ref[kv])
        bias = pe_part[0:1, :CMP_HID] + pe_part[1:2, CMP_HID:]
        hid = []
        for g in range(KV_HEADS):
            part = _dot(jnp.concatenate(lhs[g], axis=1), w1_ref[kv])
            hid.append(part[:, :CMP_HID] + pltpu.roll(part[:, CMP_HID:], n_chunks - 1, axis=0) + bias)
        act = _silu(jnp.concatenate(hid, axis=1)).astype(BF16)
        out.append(jnp.where(keep, _dot(act, w2_ref[...]), 0.0))
    return out


def _compress_prompt_body(xt_ref, pef_ref, w1_ref, w2k_ref, w2v_ref, kct_ref, vc_ref, rows_ref,
                          *, n_chunks, t_len):
    for kv in range(2):
        for c in range(t_len // LANES):
            _store_chunks(rows_ref, kv, c * (LANES // CMP_STRIDE),
                          xt_ref[0, kv * KV_LANES:(kv + 1) * KV_LANES, c * LANES:(c + 1) * LANES].T)
    kc, vc = _compress(rows_ref, n_chunks, pef_ref, w1_ref, w2k_ref, w2v_ref)
    kct_ref[0] = kc.T.astype(BF16)
    vc_ref[0] = vc.astype(BF16)


def _compress_prompt(rct, cw):
    b, _, t = rct.shape
    n_chunks = t // CMP_STRIDE
    out = pl.BlockSpec((1, n_chunks, LANES), lambda i: (i, 0, 0))
    return pl.pallas_call(
        functools.partial(_compress_prompt_body, n_chunks=n_chunks, t_len=t),
        grid=(b,),
        in_specs=[pl.BlockSpec((1, KV_ROW, t), lambda i: (i, 0, 0))] + [_const_spec(w.shape) for w in cw],
        out_specs=[out, out],
        out_shape=[jax.ShapeDtypeStruct((b, n_chunks, LANES), BF16)] * 2,
        scratch_shapes=[pltpu.VMEM((2, n_chunks * CHUNK_PITCH, LANES), F32)],
        compiler_params=_cparams(1),
        name="compress_prompt",
    )(rct, *cw)


def _softmax_rows(s):
    e = jnp.exp(s - jnp.max(s, axis=-1, keepdims=True))
    return e / jnp.sum(e, axis=-1, keepdims=True)


def _split_hi_lo(x):
    hi = x.astype(BF16)
    return hi, (x - hi.astype(F32)).astype(BF16)


def _nsa_prompt_body(q_ref, gn_ref, kct_ref, vc_ref, ks_ref, kw_ref, e_ref, c2st_ref, o_ref,
                     qaug_ref, p_ref, m_ref, l_ref, acc_ref, alpha_ref, out_ref, *, tq, kt, t_len):
    n_slots = N_HEADS
    rows = n_slots * tq
    n_sel = t_len // SEL_BLK
    start = pl.program_id(1) * tq
    lane = lax.broadcasted_iota(jnp.int32, (tq, LANES), 1)
    t_pos = start + lax.broadcasted_iota(jnp.int32, (tq, LANES), 0)
    low = lane < HEAD_DIM
    gn = gn_ref[0]
    slot_rows = [slice(slot * tq, (slot + 1) * tq) for slot in range(n_slots)]

    def gate(slot, branch):
        v, half = divmod(slot, KV_HEADS)
        c = (half * HPG + v) * N_BRANCH + branch
        return gn[:, c:c + 1]

    q = q_ref[0].astype(F32)
    for slot, rs in enumerate(slot_rows):
        v, half = divmod(slot, KV_HEADS)
        qv = q[:, v * LANES:(v + 1) * LANES]
        qaug_ref[rs, :LANES] = (jnp.where(low, qv, 0.0) if half == 0 else jnp.where(low, 0.0, qv)).astype(BF16)

    cmp_ok = lane * CMP_STRIDE + (CMP_BLK - 1) <= t_pos
    s_c = _dot(qaug_ref[:, :LANES], kct_ref[0])
    p_c = jnp.concatenate(
        [jnp.where(cmp_ok, _softmax_rows(jnp.where(cmp_ok, s_c[rs], NEG)), 0.0) for rs in slot_rows], axis=0)
    o_c = _dot(p_c.astype(BF16), vc_ref[0])
    for slot, rs in enumerate(slot_rows):
        out_ref[rs, :] = gate(slot, 0) * o_c[rs]

    blk = lax.broadcasted_iota(jnp.int32, (n_sel, tq), 0)
    cur = _div(start + lax.broadcasted_iota(jnp.int32, (n_sel, tq), 1), SEL_BLK)
    valid = blk <= cur
    forced = valid & ((blk == 0) | (blk == cur) | (blk == cur - 1))
    for half in range(KV_HEADS):
        psum = p_c[slot_rows[half]]
        for v in range(1, HPG):
            psum = psum + p_c[slot_rows[v * KV_HEADS + half]]
        hi, lo = _split_hi_lo(psum)
        imp = (_dot_nt(c2st_ref[...], hi) + _dot_nt(c2st_ref[...], lo))[:n_sel]
        score = jnp.where(forced, BIG, jnp.where(valid, imp, -BIG))
        rank = jnp.zeros((n_sel, tq), F32)
        for i in range(n_sel):
            r = score[i:i + 1, :]
            tie = jnp.where(blk > i, 1.0, 0.0)
            rank = rank + jnp.where(r > score, 1.0, jnp.where(r == score, tie, 0.0))
        pen_t = jnp.concatenate([jnp.where(rank >= TOP_N, NEG, 0.0), jnp.zeros((LANES - n_sel, tq), F32)], axis=0)
        pen = pen_t.T.astype(BF16)
        for v in range(HPG):
            qaug_ref[slot_rows[v * KV_HEADS + half], LANES:] = pen

    def reset():
        m_ref[...] = jnp.full((rows, LANES), NEG, F32)
        l_ref[...] = jnp.zeros((rows, LANES), F32)
        acc_ref[...] = jnp.zeros((rows, LANES), F32)

    d = lax.broadcasted_iota(jnp.int32, (tq, kt), 1) - lax.broadcasted_iota(jnp.int32, (tq, kt), 0)
    visible = {"causal": d <= 0, "lower": d > 0}

    def tile_step(kv_ref, koff, select, masks, only_step=False):
        w = len(masks) * kt
        kv = kv_ref[0, :, pl.ds(koff, w)]
        if select:
            s_all = _dot(qaug_ref[...], jnp.concatenate([kv[:KV_LANES], e_ref[:, pl.ds(koff, w)]], axis=0))
        else:
            s_all = _dot(qaug_ref[:, :LANES], kv[:KV_LANES])
        for rs in slot_rows:
            s = jnp.concatenate(
                [s_all[rs, jb * kt:(jb + 1) * kt] if mask is None
                 else jnp.where(visible[mask], s_all[rs, jb * kt:(jb + 1) * kt], NEG)
                 for jb, mask in enumerate(masks)], axis=1)
            if only_step:
                m_new = jnp.broadcast_to(jnp.max(s, axis=-1, keepdims=True), (tq, LANES))
            else:
                m_prev = m_ref[rs, :]
                m_new = jnp.maximum(m_prev, jnp.max(s, axis=-1, keepdims=True))
                alpha_ref[rs, :] = jnp.exp(m_prev - m_new)
                m_ref[rs, :] = m_new
            p_ref[rs, :w] = jnp.exp(s - jnp.concatenate([m_new] * (w // LANES), axis=1)).astype(BF16)
        pv = _dot_nt(p_ref[:, :w], jnp.concatenate([kv[KV_LANES:], jnp.ones((KV_LANES, w), BF16)], axis=0))
        if only_step:
            return pv
        acc_ref[...] = alpha_ref[...] * acc_ref[...] + pv[:, :LANES]
        l_ref[...] = alpha_ref[...] * l_ref[...] + pv[:, LANES:]

    def add_branch(branch, acc, den):
        for slot, rs in enumerate(slot_rows):
            out_ref[rs, :] = out_ref[rs, :] + gate(slot, branch) * (acc[rs, :] / den[rs, :])

    reset()
    n_full = start // kt

    def sel_body(j, carry):
        tile_step(ks_ref, pl.multiple_of(j * 2 * kt, 2 * kt), True, (None, None))
        return carry

    lax.fori_loop(0, n_full // 2, sel_body, 0)

    @pl.when(n_full % 2 == 1)
    def _():
        tile_step(ks_ref, pl.multiple_of(start - kt, kt), True, (None, "causal"))

    @pl.when(n_full % 2 == 0)
    def _():
        tile_step(ks_ref, pl.multiple_of(start, kt), True, ("causal",))
    add_branch(1, acc_ref, l_ref)

    n_win = WINDOW // kt
    for n_before in range(n_win + 1):
        masks = [None] * n_before + ["causal"]
        if n_before == n_win:
            masks[0] = "lower"

        @pl.when(start >= WINDOW if n_before == n_win else start == n_before * kt)
        def _():
            pv = tile_step(kw_ref, pl.multiple_of(start - n_before * kt, kt), False, masks, only_step=True)
            add_branch(2, pv[:, :LANES], pv[:, LANES:])

    o_ref[0] = jnp.concatenate(
        [jnp.where(low, out_ref[slot_rows[2 * v], :], out_ref[slot_rows[2 * v + 1], :]) for v in range(HPG)],
        axis=1).astype(BF16)


def _nsa_prompt(q, gn, kct, vc, kst, kwt, e_t, c2s_t):
    b, t, _ = q.shape
    tq = kt = KEY_TILE
    assert WINDOW % kt == 0 and t % kt == 0
    n_cmp = vc.shape[1]
    rows = N_HEADS * tq
    tile = lambda n: pl.BlockSpec((1, tq, n), lambda i, j: (i, j, 0))
    per_b = lambda r, n: pl.BlockSpec((1, r, n), lambda i, j: (i, 0, 0))
    rows_f32 = pltpu.VMEM((rows, LANES), F32)
    return pl.pallas_call(
        functools.partial(_nsa_prompt_body, tq=tq, kt=kt, t_len=t),
        grid=(b, t // tq),
        in_specs=[tile(D_ATTN), tile(LANES), per_b(LANES, n_cmp), per_b(n_cmp, LANES), per_b(KV_ROW, t),
                  per_b(KV_ROW, t), _const_spec(e_t.shape), _const_spec(c2s_t.shape)],
        out_specs=tile(D_ATTN),
        out_shape=jax.ShapeDtypeStruct((b, t, D_ATTN), BF16),
        scratch_shapes=[pltpu.VMEM((rows, 2 * LANES), BF16), pltpu.VMEM((rows, max(2 * kt, WINDOW + kt)), BF16), rows_f32, rows_f32,
                        rows_f32, rows_f32, rows_f32],
        compiler_params=_cparams(2),
        name="nsa_prompt",
    )(q, gn, kct, vc, kst, kwt, e_t, c2s_t)


def _merge_body(x_ref, g_ref, u_ref, um1_ref, um2_ref, o_ref, wb_ref, cw_ref, wco_ref, wao_ref, wmo_ref,
                y_ref, *, tm, tiles_per_seq):
    x = x_ref[...]
    h = _rms(x, g_ref[...]).astype(BF16)
    ub = _dot(h, wb_ref[:, :D_CONV])
    u = u_ref[...]
    if tiles_per_seq is None:
        um1 = um1_ref[...]
        um2 = um2_ref[...]
    else:
        row = lax.broadcasted_iota(jnp.int32, (tm, D_CONV), 0)
        keep = jnp.where(pl.program_id(0) % tiles_per_seq == 0, 0.0, 1.0)
        prev = um1_ref[...] * keep
        um1 = jnp.where(row == 0, prev[7:8, :], pltpu.roll(u, 1, axis=0))
        um2 = jnp.where(row == 0, prev[6:7, :], jnp.where(row == 1, prev[7:8, :], pltpu.roll(u, 2, axis=0)))
    conv = cw_ref[0:1, :] * um2 + cw_ref[1:2, :] * um1 + cw_ref[2:3, :] * u
    y_c = _dot((ub * conv).astype(BF16), wco_ref[...])
    y_a = _dot(o_ref[...], wao_ref[...])
    g_c = jax.nn.sigmoid(_dot(h, wb_ref[:, D_CONV:D_CONV + D_MODEL]))
    g_a = jax.nn.sigmoid(_dot(h, wb_ref[:, D_CONV + D_MODEL:]))
    y_ref[...] = x + _dot((g_c * y_c + g_a * y_a).astype(BF16), wmo_ref[...])


def _merge(x, g, uin, um1, um2, o, wb, cw, wco, wao, wmo, *, seq_len):
    m = x.shape[0]
    tm = min(ROW_TILE, m)
    row = lambda n: pl.BlockSpec((tm, n), lambda i: (i, 0))
    if seq_len is None:
        prev_specs = [row(D_CONV), row(D_CONV)]
        tiles_per_seq = None
    else:
        halo = pl.BlockSpec((8, D_CONV), lambda i: (jnp.maximum(i * (tm // 8) - 1, 0), 0))
        prev_specs = [halo, halo]
        tiles_per_seq = seq_len // tm
    return pl.pallas_call(
        functools.partial(_merge_body, tm=tm, tiles_per_seq=tiles_per_seq),
        grid=(m // tm,),
        in_specs=[row(D_MODEL), _const_spec((1, D_MODEL)), row(D_CONV)] + prev_specs +
                 [row(D_ATTN), _const_spec(wb.shape), _const_spec(cw.shape), _const_spec(wco.shape),
                  _const_spec(wao.shape), _const_spec(wmo.shape)],
        out_specs=row(D_MODEL),
        out_shape=jax.ShapeDtypeStruct((m, D_MODEL), F32),
        compiler_params=_cparams(1),
        name="merge",
    )(x, g, uin, um1, um2, o, wb, cw, wco, wao, wmo)


def _expand_q(q_ref):
    row = lax.broadcasted_iota(jnp.int32, (N_HEADS, LANES), 0)
    lane = lax.broadcasted_iota(jnp.int32, (N_HEADS, LANES), 1)
    q = q_ref[0].astype(F32)
    qe = jnp.zeros((N_HEADS, LANES), F32)
    for v in range(HPG):
        qv = jnp.broadcast_to(q[:, v * LANES:(v + 1) * LANES], (N_HEADS, LANES))
        qe = jnp.where((_div(row, KV_HEADS) == v) & (_div(lane, HEAD_DIM) == _mod(row, KV_HEADS)), qv, qe)
    return qe


def _sample_cmp_body(pt_ref, cache_ref, q_ref, pef_ref, w1_ref, w2k_ref, w2v_ref, c2s_ref,
                     ocmp_ref, idx_ref, buf_ref, rows_ref, sem_ref, *, n_pages, page, t_pos):
    b = pl.program_id(0)
    n_b = pl.num_programs(0)
    n_chunks = n_pages * page // CMP_STRIDE
    n_sel = t_pos // SEL_BLK + 1

    def page_copy(bb, p, slot):
        return pltpu.make_async_copy(cache_ref.at[pt_ref[bb * n_pages + p]], buf_ref.at[slot, p], sem_ref.at[slot])

    def start_all(bb, slot):
        for p in range(n_pages):
            page_copy(bb, p, slot).start()

    @pl.when(b == 0)
    def _():
        start_all(0, 0)

    @pl.when(b + 1 < n_b)
    def _():
        start_all(b + 1, (b + 1) % 2)

    slot = b % 2
    for p in range(n_pages):
        page_copy(b, p, slot).wait()

    group = 8
    for p0 in range(0, n_pages, group):
        for kv in range(2):
            wide = jnp.concatenate([buf_ref[slot, p0 + i, kv] for i in range(group)], axis=1)
            _store_chunks(rows_ref, kv, p0 * page // CMP_STRIDE, wide.T)
    kc, vc = _compress(rows_ref, n_chunks, pef_ref, w1_ref, w2k_ref, w2v_ref)

    lane = lax.broadcasted_iota(jnp.int32, (N_HEADS, LANES), 1)
    qe = _expand_q(q_ref)
    cidx = lax.broadcasted_iota(jnp.int32, (N_HEADS, n_chunks), 1)
    cmp_ok = cidx * CMP_STRIDE + (CMP_BLK - 1) <= t_pos
    p = jnp.where(cmp_ok, _softmax_rows(jnp.where(cmp_ok, _dot_nt(qe.astype(BF16), kc.astype(BF16)), NEG)), 0.0)
    ocmp_ref[0] = _dot(p.astype(BF16), vc.astype(BF16))

    hi, lo = _split_hi_lo(p)
    imp_h = _dot(hi, c2s_ref[...]) + _dot(lo, c2s_ref[...])
    n_lanes = c2s_ref.shape[1]
    grow = lax.broadcasted_iota(jnp.int32, (N_HEADS, n_lanes), 0)
    glane = lax.broadcasted_iota(jnp.int32, (N_HEADS, n_lanes), 1)
    imp = jnp.zeros((N_HEADS, n_lanes), F32)
    for g in range(KV_HEADS):
        tot = jnp.sum(jnp.where(_mod(grow, KV_HEADS) == g, imp_h, 0.0), axis=0, keepdims=True)
        imp = jnp.where(grow == g, jnp.broadcast_to(tot, imp.shape), imp)
    cur = t_pos // SEL_BLK
    forced = (glane == 0) | (glane == cur) | (glane == cur - 1)
    score = jnp.where(glane < n_sel, jnp.where(forced, BIG, imp), GONE)
    rank = jnp.zeros((N_HEADS, n_lanes), F32)
    for i in range(n_sel):
        col = score[:, i:i + 1]
        tie = jnp.where(glane > i, 1.0, 0.0)
        rank = rank + jnp.where(col > score, 1.0, jnp.where(col == score, tie, 0.0))
    flane = glane.astype(F32)
    picked = jnp.zeros((N_HEADS, LANES), jnp.int32)
    for k in range(TOP_N):
        block_k = jnp.sum(jnp.where(rank == float(k), flane, 0.0), axis=-1, keepdims=True)
        picked = jnp.where(lane == k, block_k.astype(jnp.int32), picked)
    idx_ref[0] = picked


def _sample_cmp(page_table, cache, q, cw, c2s, *, t_pos):
    bsz, n_pages = page_table.shape
    page = cache.shape[-1]
    assert page == LANES
    gs = pltpu.PrefetchScalarGridSpec(
        num_scalar_prefetch=1,
        grid=(bsz,),
        in_specs=[pl.BlockSpec(memory_space=pl.ANY), pl.BlockSpec((1, 1, D_ATTN), lambda i, pt: (i, 0, 0))] +
                 [pl.BlockSpec(w.shape, (lambda nd: lambda i, pt: (0,) * nd)(w.ndim), pipeline_mode=pl.Buffered(1))
                  for w in (*cw, c2s)],
        out_specs=[pl.BlockSpec((1, N_HEADS, LANES), lambda i, pt: (i, 0, 0))] * 2,
        scratch_shapes=[pltpu.VMEM((2, n_pages, 2, KV_LANES, page), F32),
                        pltpu.VMEM((2, n_pages * page // CMP_STRIDE * CHUNK_PITCH, KV_LANES), F32),
                        pltpu.SemaphoreType.DMA((2,))],
    )
    return pl.pallas_call(
        functools.partial(_sample_cmp_body, n_pages=n_pages, page=page, t_pos=t_pos),
        grid_spec=gs,
        out_shape=[jax.ShapeDtypeStruct((bsz, N_HEADS, LANES), F32),
                   jax.ShapeDtypeStruct((bsz, N_HEADS, LANES), jnp.int32)],
        compiler_params=_cparams(1),
        name="sample_cmp",
    )(page_table.reshape(-1), cache, q, *cw, c2s)


def _sample_attn_body(pt_ref, idx_ref, cache_ref, q_ref, gn_ref, ns_ref, nwt_ref, win_ref, ocmp_ref,
                      o_ref, wout_ref, buf_ref, sem_ref, *, n_pages, page, t_pos):
    b = pl.program_id(0)
    n_b = pl.num_programs(0)
    halves = page // SEL_BLK
    new_blk = t_pos // SEL_BLK
    n_slot = KV_HEADS * TOP_N

    def blk_of(bb, k):
        return idx_ref[bb * n_slot + k]

    def page_copy(bb, k, slot):
        j = jnp.minimum(blk_of(bb, k), new_blk - 1)
        return pltpu.make_async_copy(cache_ref.at[pt_ref[bb * n_pages + j // halves]], buf_ref.at[slot, k],
                                     sem_ref.at[slot])

    def start_all(bb, slot):
        for k in range(n_slot):
            page_copy(bb, k, slot).start()

    @pl.when(b == 0)
    def _():
        start_all(0, 0)

    @pl.when(b + 1 < n_b)
    def _():
        start_all(b + 1, (b + 1) % 2)

    slot = b % 2
    for k in range(n_slot):
        page_copy(b, k, slot).wait()

    row = lax.broadcasted_iota(jnp.int32, (N_HEADS, LANES), 0)
    lane = lax.broadcasted_iota(jnp.int32, (N_HEADS, LANES), 1)
    grp0 = _mod(row, KV_HEADS) == 0
    qe = _expand_q(q_ref)
    qb = qe.astype(BF16)

    n_keys = TOP_N * page
    klane = lax.broadcasted_iota(jnp.int32, (N_HEADS, n_keys), 1)
    s_g = []
    for g in range(KV_HEADS):
        kt_g = jnp.concatenate([buf_ref[slot, g * TOP_N + k, 0] for k in range(TOP_N)], axis=1).astype(BF16)
        pen = jnp.full((N_HEADS, n_keys), NEG, F32)
        for k in range(TOP_N):
            j = blk_of(b, g * TOP_N + k)
            lo = k * page + (j % halves) * SEL_BLK
            seen = jnp.where(j >= new_blk, NEG, 0.0)
            pen = jnp.where((klane >= lo) & (klane < lo + SEL_BLK), seen, pen)
        s_g.append(_dot(qb, kt_g) + pen)
    grp0_k = _mod(lax.broadcasted_iota(jnp.int32, (N_HEADS, n_keys), 0), KV_HEADS) == 0
    s = jnp.where(grp0_k, s_g[0], s_g[1])
    r_new = ns_ref[0].astype(BF16).astype(F32)
    k_new, v_new = r_new[:, :KV_LANES], r_new[:, KV_LANES:]
    s_new = jnp.sum(qe * k_new, axis=-1, keepdims=True)
    m = jnp.maximum(jnp.max(s, axis=-1, keepdims=True), s_new)
    p = jnp.exp(s - m)
    p_new = jnp.exp(s_new - m)
    den = jnp.sum(p, axis=-1, keepdims=True) + p_new
    pb = p.astype(BF16)
    pv = []
    for g in range(KV_HEADS):
        vt_g = jnp.concatenate([buf_ref[slot, g * TOP_N + k, 1] for k in range(TOP_N)], axis=1).astype(BF16)
        pv.append(_dot_nt(pb, vt_g))
    o_sel = (jnp.where(grp0, pv[0], pv[1]) + p_new.astype(BF16).astype(F32) * v_new) / den

    w_len = win_ref.shape[2]
    nw = nwt_ref[0]
    pick = lax.broadcasted_iota(jnp.int32, nw.shape, 1) == b
    new_col = jnp.sum(jnp.where(pick, nw, 0.0), axis=-1, keepdims=True)
    wlane = lax.broadcasted_iota(jnp.int32, (KV_ROW, w_len), 1)
    wout = jnp.where(wlane == w_len - 1, new_col, pltpu.roll(win_ref[0], w_len - 1, axis=1))
    wout_ref[0] = wout
    sw = _dot(qb, wout[:KV_LANES].astype(BF16))
    ew = jnp.exp(sw - jnp.max(sw, axis=-1, keepdims=True))
    o_win = _dot_nt(ew.astype(BF16), wout[KV_LANES:].astype(BF16)) / jnp.sum(ew, axis=-1, keepdims=True)

    gn = jnp.broadcast_to(gn_ref[0], (N_HEADS, LANES))
    head = _mod(row, KV_HEADS) * HPG + _div(row, KV_HEADS)

    def gate(branch):
        return jnp.sum(jnp.where(lane == head * N_BRANCH + branch, gn, 0.0), axis=-1, keepdims=True)

    out = gate(0) * ocmp_ref[0] + gate(1) * o_sel + gate(2) * o_win
    o_ref[0] = jnp.concatenate(
        [jnp.sum(jnp.where((_div(row, KV_HEADS) == v) & (_div(lane, HEAD_DIM) == _mod(row, KV_HEADS)), out, 0.0),
                 axis=0, keepdims=True)
         for v in range(HPG)], axis=1).astype(BF16)


def _sample_attn(page_table, idx, cache, q, gn, new_sel, new_win_t, win_t, ocmp, *, t_pos):
    bsz, n_pages = page_table.shape
    page = cache.shape[-1]
    w_len = win_t.shape[2]
    one = lambda n: pl.BlockSpec((1, 1, n), lambda i, *_: (i, 0, 0))
    win_spec = pl.BlockSpec((1, KV_ROW, w_len), lambda i, *_: (i, 0, 0))
    gs = pltpu.PrefetchScalarGridSpec(
        num_scalar_prefetch=2,
        grid=(bsz,),
        in_specs=[pl.BlockSpec(memory_space=pl.ANY), one(D_ATTN), one(LANES), one(KV_ROW),
                  pl.BlockSpec(new_win_t.shape, lambda i, *_: (0, 0, 0), pipeline_mode=pl.Buffered(1)), win_spec,
                  pl.BlockSpec((1, N_HEADS, LANES), lambda i, *_: (i, 0, 0))],
        out_specs=[one(D_ATTN), win_spec],
        scratch_shapes=[pltpu.VMEM((2, KV_HEADS * TOP_N, 2, KV_LANES, page), F32), pltpu.SemaphoreType.DMA((2,))],
    )
    return pl.pallas_call(
        functools.partial(_sample_attn_body, n_pages=n_pages, page=page, t_pos=t_pos),
        grid_spec=gs,
        out_shape=[jax.ShapeDtypeStruct((bsz, 1, D_ATTN), BF16), jax.ShapeDtypeStruct(win_t.shape, F32)],
        compiler_params=_cparams(1),
        name="sample_attn",
    )(page_table.reshape(-1), idx.reshape(-1), cache, q, gn, new_sel, new_win_t, win_t, ocmp)


def _rope_tables(pos):
    half = HEAD_DIM // 2
    freqs = jnp.power(ROPE_THETA, -jnp.arange(half, dtype=F32) * (2.0 / HEAD_DIM))
    ang = pos.astype(F32)[:, None] * freqs[None, :]
    cos, sin = jnp.cos(ang), jnp.sin(ang)
    reps = LANES // HEAD_DIM
    return (jnp.tile(jnp.concatenate([cos, cos], axis=1), (1, reps)),
            jnp.tile(jnp.concatenate([-sin, sin], axis=1), (1, reps)), cos.T, sin.T)


def _compress_weights(w1k, w2k, pek, w1v, w2v, pev):
    flat = CMP_STRIDE * HEAD_DIM

    def pe_rows(pe):
        return jnp.pad(pe.reshape(CMP_RATIO, flat), ((0, 8 - CMP_RATIO), (0, 0)))

    def first(w1):
        return w1.reshape(CMP_RATIO, flat, CMP_HID).transpose(1, 0, 2).reshape(flat, CMP_RATIO * CMP_HID)

    def pair(w2):
        z = jnp.zeros_like(w2)
        return jnp.concatenate([jnp.concatenate([w2, z], axis=1), jnp.concatenate([z, w2], axis=1)],
                               axis=0).astype(BF16)

    return (jnp.stack([pe_rows(pek), pe_rows(pev)]).astype(BF16), jnp.stack([first(w1k), first(w1v)]).astype(BF16),
            pair(w2k), pair(w2v))


def _cmp_to_sel(n_cmp_pad, n_sel_pad, n_cmp, n_sel):
    i = np.arange(n_cmp_pad)[:, None]
    j = np.arange(n_sel_pad)[None, :]
    start = i * CMP_STRIDE
    return (start < j * SEL_BLK + SEL_BLK) & (start + CMP_BLK > j * SEL_BLK) & (i < n_cmp) & (j < n_sel)


def _kv_out(rows_t):
    n, _, t = rows_t.shape
    return rows_t.reshape(1, n, 2, KV_HEADS, HEAD_DIM, t).transpose(0, 1, 5, 2, 3, 4)


def _kv_in(a):
    n, t = a.shape[:2]
    return a.transpose(0, 2, 3, 4, 1).reshape(n, 2, KV_LANES, t)


def kernel(x_prompt, x_sample, cache_cmp_kv, cache_sel_kv, state_win_kv, state_conv, page_table, norm_ffn1, ffn1_w_gate, ffn1_w_up, ffn1_w_down, norm_mix, w_mix_in, conv_w, w_cmpk1, w_cmpk2, pe_cmpk, w_cmpv1, w_cmpv2, pe_cmpv, w_conv_out, w_attn_out, w_mix_out, norm_ffn2, ffn2_w_gate, ffn2_w_up, ffn2_w_down, norm_final):
    depth = norm_ffn1.shape[0]
    assert depth == 1
    bsz, t_len, _ = x_prompt.shape
    dbsz, dseq, _ = x_sample.shape
    n_pages = page_table.shape[1]
    page = cache_cmp_kv.shape[2]
    past = n_pages * page
    w_len = state_win_kv.shape[2]
    assert dseq == 1 and w_len == WINDOW and past >= WINDOW and past % SEL_BLK == 0 and page % SEL_BLK == 0
    assert t_len >= WINDOW and t_len // CMP_STRIDE == LANES

    offs = np.cumsum(IN_SPLITS)[:-1].tolist()
    w_ub, w_uc, w_ux, w_q, w_kvc, w_kvs, w_kvw, w_gn, w_gb = jnp.split(w_mix_in[0], offs, axis=1)
    order = np.asarray(HEAD_ORDER)
    w_q = w_q.reshape(D_MODEL, N_HEADS, HEAD_DIM)[:, order].reshape(D_MODEL, D_ATTN)
    w_gn = jnp.pad(w_gn, ((0, 0), (0, LANES - w_gn.shape[1])))
    w_r = jnp.concatenate([w_uc, w_ux, w_q, w_gn], axis=1).astype(BF16)
    w_r_s = jnp.concatenate([w_uc, w_ux, w_q, w_gn, w_kvs, w_kvw], axis=1).astype(BF16)
    w_t = jnp.concatenate([w_kvc, w_kvs, w_kvw], axis=1).T.astype(BF16)
    w_b = jnp.concatenate([w_ub, w_gb], axis=1).astype(BF16)
    w_ao = w_attn_out[0].reshape(N_HEADS, HEAD_DIM, D_MODEL)[order].reshape(D_ATTN, D_MODEL).astype(BF16)
    w_co = w_conv_out[0].astype(BF16)
    w_mo = w_mix_out[0].astype(BF16)
    cw = _compress_weights(w_cmpk1[0], w_cmpk2[0], pe_cmpk[0], w_cmpv1[0], w_cmpv2[0], pe_cmpv[0])
    ffn1 = (norm_ffn1, ffn1_w_gate[0].astype(BF16), ffn1_w_up[0].astype(BF16), ffn1_w_down[0].astype(BF16))
    ffn2 = (norm_ffn2, ffn2_w_gate[0].astype(BF16), ffn2_w_up[0].astype(BF16), ffn2_w_down[0].astype(BF16))
    g_final = norm_final.reshape(1, D_MODEL)

    xp = _ffn(x_prompt.reshape(bsz * t_len, D_MODEL), *ffn1, g_final, final_norm=False)
    tabs = _rope_tables(jnp.arange(t_len, dtype=jnp.int32))
    uin, q, gn, rct, rst, rwt, kst, kwt = _mixin(xp.reshape(bsz, t_len, D_MODEL), norm_mix, w_r, w_t, tabs,
                                                 rows_out=False)
    kct, vc = _compress_prompt(rct, cw)
    n_chunks = t_len // CMP_STRIDE
    n_sel = t_len // SEL_BLK
    e_t = jnp.asarray(np.arange(LANES)[:, None] == np.arange(t_len)[None, :] // SEL_BLK, BF16)
    c2s_t = jnp.asarray(_cmp_to_sel(n_chunks, LANES, n_chunks - CMP_RATIO + 1, n_sel).T, BF16)
    o = _nsa_prompt(q, gn, kct, vc, kst, kwt, e_t, c2s_t)
    uin = uin.reshape(bsz * t_len, D_CONV)
    xp = _merge(xp, norm_mix, uin, uin, uin, o.reshape(bsz * t_len, D_ATTN), w_b, conv_w[0], w_co, w_ao, w_mo,
                seq_len=t_len)
    y_prompt = _ffn(xp, *ffn2, g_final, final_norm=True).reshape(bsz, t_len, D_MODEL)
    cmp_p = _kv_out(rct)
    sel_p = _kv_out(rst)
    win_p = _kv_out(rwt[:, :, t_len - WINDOW:])
    conv_p = uin.reshape(1, bsz, t_len, D_CONV)[:, :, t_len - (CONV_W - 1):]

    xs = _ffn(x_sample.reshape(dbsz, D_MODEL), *ffn1, g_final, final_norm=False)
    tabs_s = _rope_tables(jnp.full((dbsz,), past, jnp.int32))
    uin_s, q_s, gn_s, rct_s, rst_s, rwt_s, _, _, rs_s, _ = _mixin(xs.reshape(1, dbsz, D_MODEL), norm_mix, w_r_s,
                                                                  w_t, tabs_s, rows_out=True)
    one = lambda a: a.reshape(dbsz, 1, a.shape[-1])
    n_chunks_s = past // CMP_STRIDE
    n_sel_s = past // SEL_BLK + 1
    n_sel_pad = -(-n_sel_s // LANES) * LANES
    c2s_s = jnp.asarray(_cmp_to_sel(n_chunks_s, n_sel_pad, (past + 1) // CMP_STRIDE - CMP_RATIO + 1, n_sel_s), BF16)
    ocmp, idx = _sample_cmp(page_table, _kv_in(cache_cmp_kv[0]), one(q_s), cw, c2s_s, t_pos=past)
    idx = idx[:, :KV_HEADS, :TOP_N]
    win_t = _kv_in(state_win_kv[0]).reshape(dbsz, KV_ROW, w_len)
    o_s, win_s = _sample_attn(page_table, idx, _kv_in(cache_sel_kv[0]), one(q_s), one(gn_s), one(rs_s), rwt_s,
                              win_t, ocmp, t_pos=past)
    uin_s = uin_s.reshape(dbsz, D_CONV)
    xs = _merge(xs, norm_mix, uin_s, state_conv[0, :, 1], state_conv[0, :, 0], o_s.reshape(dbsz, D_ATTN), w_b,
                conv_w[0], w_co, w_ao, w_mo, seq_len=None)
    y_sample = _ffn(xs, *ffn2, g_final, final_norm=True).reshape(dbsz, 1, D_MODEL)
    new_out = lambda rt: rt.reshape(1, 2, KV_HEADS, HEAD_DIM, dbsz).transpose(0, 4, 1, 2, 3)[:, :, None]
    cmp_s = new_out(rct_s)
    sel_s = new_out(rst_s)
    win_s = _kv_out(win_s)
    conv_s = jnp.concatenate([state_conv[0, :, 1:], uin_s[:, None, :]], axis=1)[None]
    return (y_prompt, y_sample, cmp_p, sel_p, win_p, conv_p, cmp_s, sel_s, win_s, conv_s)
```

```python
import functools

import numpy as np
import jax
import jax.numpy as jnp
from jax import lax
from jax.experimental import pallas as pl
from jax.experimental.pallas import tpu as pltpu

F32 = jnp.float32
BF16 = jnp.bfloat16

D_MODEL = 1024
D_CONV = 512
CONV_W = 3
N_HEADS = 8
KV_HEADS = 2
HPG = N_HEADS // KV_HEADS
HEAD_DIM = 64
D_ATTN = N_HEADS * HEAD_DIM
KV_ROW = 2 * KV_HEADS * HEAD_DIM
CMP_STRIDE = 16
CMP_RATIO = 2
CMP_BLK = CMP_STRIDE * CMP_RATIO
CMP_HID = 128
SEL_BLK = 64
TOP_N = 16
WINDOW = 512
N_BRANCH = 3
D_FF = 2816
ROPE_THETA = 10000.0
EPS = 1e-6
NEG = -1e30
BIG = 1e9
GONE = -3e38
IN_SPLITS = (D_CONV, D_CONV, D_CONV, D_ATTN, KV_ROW, KV_ROW, KV_ROW, N_HEADS * N_BRANCH, 2 * D_MODEL)

LANES = 128
KV_LANES = KV_HEADS * HEAD_DIM
assert KV_LANES == LANES and KV_HEADS == 2 and 2 * HEAD_DIM == LANES
HEAD_ORDER = tuple(half * HPG + v for v in range(HPG) for half in range(KV_HEADS))
VMEM_LIMIT = 56 * 1024 * 1024
ROW_TILE = 512
KEY_TILE = 256


def _cparams(n_grid):
    return pltpu.CompilerParams(dimension_semantics=("arbitrary",) * n_grid, vmem_limit_bytes=VMEM_LIMIT)


def _const_spec(shape):
    nd = len(shape)
    return pl.BlockSpec(shape, lambda *_: (0,) * nd, pipeline_mode=pl.Buffered(1))


def _rms(x, g):
    return x * lax.rsqrt(jnp.mean(x * x, axis=-1, keepdims=True) + EPS) * g


def _silu(x):
    return x * jax.nn.sigmoid(x)


def _div(x, n):
    return lax.shift_right_logical(x, int(np.log2(n)))


def _mod(x, n):
    return x & (n - 1)


def _dot(a, b):
    return jnp.dot(a, b, preferred_element_type=F32)


def _dot_nt(a, b):
    return lax.dot_general(a, b, (((1,), (1,)), ((), ())), preferred_element_type=F32)


def _ffn_body(x_ref, g_ref, wg_ref, wu_ref, wd_ref, gf_ref, o_ref, *, final_norm):
    x = x_ref[...]
    h = _rms(x, g_ref[...]).astype(BF16)
    a = (_silu(_dot(h, wg_ref[...])) * _dot(h, wu_ref[...])).astype(BF16)
    y = x + 0.5 * _dot(a, wd_ref[...])
    if final_norm:
        y = _rms(y, gf_ref[...])
    o_ref[...] = y


def _ffn(x, g, wg, wu, wd, gf, *, final_norm):
    m = x.shape[0]
    tm = min(ROW_TILE, m)
    row = pl.BlockSpec((tm, D_MODEL), lambda i: (i, 0))
    return pl.pallas_call(
        functools.partial(_ffn_body, final_norm=final_norm),
        grid=(m // tm,),
        in_specs=[row, _const_spec((1, D_MODEL)), _const_spec(wg.shape), _const_spec(wu.shape),
                  _const_spec(wd.shape), _const_spec((1, D_MODEL))],
        out_specs=row,
        out_shape=jax.ShapeDtypeStruct((m, D_MODEL), F32),
        compiler_params=_cparams(1),
        name="ffn",
    )(x, g, wg, wu, wd, gf)


ROW_COLS = {"uc": (0, 512), "ux": (512, 1024), "q": (1024, 1536), "gn": (1536, 1664), "kvs": (1664, 1920),
            "kvw": (1920, 2176)}


def _rope_tile(z, cos, sin):
    first = _mod(lax.broadcasted_iota(jnp.int32, z.shape, 1), HEAD_DIM) < (HEAD_DIM // 2)
    rot = jnp.where(first, pltpu.roll(z, LANES - HEAD_DIM // 2, axis=1), pltpu.roll(z, HEAD_DIM // 2, axis=1))
    return z * cos + rot * sin


def _rope_rows_t(zt, cos_t, sin_t):
    half = HEAD_DIM // 2
    out = []
    for hh in range(KV_LANES // HEAD_DIM):
        x1 = zt[hh * HEAD_DIM:hh * HEAD_DIM + half]
        x2 = zt[hh * HEAD_DIM + half:(hh + 1) * HEAD_DIM]
        out += [x1 * cos_t - x2 * sin_t, x1 * sin_t + x2 * cos_t]
    return jnp.concatenate(out, axis=0)


def _mixin_body(x_ref, g_ref, wr_ref, wt_ref, cos_ref, sin_ref, cost_ref, sint_ref,
                uin_ref, q_ref, gn_ref, rct_ref, rst_ref, rwt_ref, kst_ref, kwt_ref, *row_refs):
    h = _rms(x_ref[0], g_ref[...]).astype(BF16)
    cos = cos_ref[...]
    sin = sin_ref[...]

    def proj(name):
        lo, hi = ROW_COLS[name]
        return _dot(h, wr_ref[:, lo:hi])

    uin_ref[0] = proj("uc") * proj("ux")
    zq = proj("q")
    scale = HEAD_DIM ** -0.5
    q_ref[0] = jnp.concatenate(
        [_rope_tile(zq[:, v * LANES:(v + 1) * LANES], cos, sin) * scale for v in range(D_ATTN // LANES)],
        axis=1).astype(BF16)
    gn_ref[0] = jax.nn.sigmoid(proj("gn"))
    cos_t = cost_ref[...]
    sin_t = sint_ref[...]
    for i, (f_ref, b_ref) in enumerate(((rct_ref, None), (rst_ref, kst_ref), (rwt_ref, kwt_ref))):
        zt = _dot_nt(wt_ref[i * KV_ROW:(i + 1) * KV_ROW, :], h)
        rows_t = jnp.concatenate([_rope_rows_t(zt[:KV_LANES], cos_t, sin_t), zt[KV_LANES:]], axis=0)
        f_ref[0] = rows_t
        if b_ref is not None:
            b_ref[0] = rows_t.astype(BF16)
    for name, r_ref in zip(("kvs", "kvw"), row_refs):
        z = proj(name)
        r_ref[0] = jnp.concatenate([_rope_tile(z[:, :KV_LANES], cos, sin), z[:, KV_LANES:]], axis=1)


def _mixin(x, g, wr, wt, tabs, *, rows_out):
    b, t, _ = x.shape
    tm = min(2 * ROW_TILE, t)
    row = lambda n: pl.BlockSpec((1, tm, n), lambda i, j: (i, j, 0))
    col = pl.BlockSpec((1, KV_ROW, tm), lambda i, j: (i, 0, j))
    tab = pl.BlockSpec((tm, LANES), lambda i, j: (j, 0))
    tab_t = pl.BlockSpec((HEAD_DIM // 2, tm), lambda i, j: (0, j))
    rsd = lambda n, dt: jax.ShapeDtypeStruct((b, t, n), dt)
    csd = lambda dt: jax.ShapeDtypeStruct((b, KV_ROW, t), dt)
    extra = 2 if rows_out else 0
    return pl.pallas_call(
        _mixin_body,
        grid=(b, t // tm),
        in_specs=[row(D_MODEL), _const_spec((1, D_MODEL)), _const_spec(wr.shape), _const_spec(wt.shape),
                  tab, tab, tab_t, tab_t],
        out_specs=[row(D_CONV), row(D_ATTN), row(LANES), col, col, col, col, col] + [row(KV_ROW)] * extra,
        out_shape=[rsd(D_CONV, F32), rsd(D_ATTN, BF16), rsd(LANES, F32), csd(F32), csd(F32), csd(F32), csd(BF16),
                   csd(BF16)] + [rsd(KV_ROW, F32)] * extra,
        compiler_params=_cparams(2),
        name="mix_in",
    )(x, g, wr, wt, *tabs)


CHUNK_PITCH = 24


def _store_chunks(rows_ref, kv, first_chunk, rows):
    for j in range(rows.shape[0] // CMP_STRIDE):
        lo = (first_chunk + j) * CHUNK_PITCH
        rows_ref[kv, lo:lo + CMP_STRIDE, :] = rows[j * CMP_STRIDE:(j + 1) * CMP_STRIDE]


def _compress(rows_ref, n_chunks, pef_ref, w1_ref, w2k_ref, w2v_ref):
    keep = lax.broadcasted_iota(jnp.int32, (n_chunks, LANES), 0) < n_chunks - 1
    out = []
    for kv, w2_ref in enumerate((w2k_ref, w2v_ref)):
        lhs = jnp.concatenate([rows_ref[kv, pl.ds(s, n_chunks, stride=CHUNK_PITCH), :].astype(BF16)
                               for s in range(CMP_STRIDE)], axis=1)
        part = _dot(lhs, w1_ref[kv])
        pe_part = _dot(pef_ref[kv], w1_ref[kv])
        hid = []
        for g in range(KV_HEADS):
            c0, c1 = (CMP_RATIO * g) * CMP_HID, (CMP_RATIO * g + 1) * CMP_HID
            bias = pe_part[0:1, c0:c0 + CMP_HID] + pe_part[1:2, c1:c1 + CMP_HID]
            hid.append(part[:, c0:c0 + CMP_HID] + pltpu.roll(part[:, c1:c1 + CMP_HID], n_chunks - 1, axis=0) + bias)
        act = _silu(jnp.concatenate(hid, axis=1)).astype(BF16)
        out.append(jnp.where(keep, _dot(act, w2_ref[...]), 0.0))
    return out


def _compress_prompt_body(xt_ref, pef_ref, w1_ref, w2k_ref, w2v_ref, kct_ref, vc_ref, rows_ref,
                          *, n_chunks, t_len):
    for kv in range(2):
        for c in range(t_len // LANES):
            _store_chunks(rows_ref, kv, c * (LANES // CMP_STRIDE),
                          xt_ref[0, kv * KV_LANES:(kv + 1) * KV_LANES, c * LANES:(c + 1) * LANES].T)
    kc, vc = _compress(rows_ref, n_chunks, pef_ref, w1_ref, w2k_ref, w2v_ref)
    kct_ref[0] = kc.T.astype(BF16)
    vc_ref[0] = vc.astype(BF16)


def _compress_prompt(rct, cw):
    b, _, t = rct.shape
    n_chunks = t // CMP_STRIDE
    out = pl.BlockSpec((1, n_chunks, LANES), lambda i: (i, 0, 0))
    return pl.pallas_call(
        functools.partial(_compress_prompt_body, n_chunks=n_chunks, t_len=t),
        grid=(b,),
        in_specs=[pl.BlockSpec((1, KV_ROW, t), lambda i: (i, 0, 0))] + [_const_spec(w.shape) for w in cw],
        out_specs=[out, out],
        out_shape=[jax.ShapeDtypeStruct((b, n_chunks, LANES), BF16)] * 2,
        scratch_shapes=[pltpu.VMEM((2, n_chunks * CHUNK_PITCH, LANES), F32)],
        compiler_params=_cparams(1),
        name="compress_prompt",
    )(rct, *cw)


def _softmax_rows(s):
    e = jnp.exp(s - jnp.max(s, axis=-1, keepdims=True))
    return e / jnp.sum(e, axis=-1, keepdims=True)


def _split_hi_lo(x):
    hi = x.astype(BF16)
    return hi, (x - hi.astype(F32)).astype(BF16)


def _nsa_prompt_body(q_ref, gn_ref, kct_ref, vc_ref, ks_ref, kw_ref, e_ref, c2st_ref, o_ref,
                     qaug_ref, p_ref, m_ref, l_ref, acc_ref, alpha_ref, out_ref, *, tq, kt, t_len):
    n_slots = N_HEADS
    rows = n_slots * tq
    n_sel = t_len // SEL_BLK
    start = pl.program_id(1) * tq
    lane = lax.broadcasted_iota(jnp.int32, (tq, LANES), 1)
    t_pos = start + lax.broadcasted_iota(jnp.int32, (tq, LANES), 0)
    low = lane < HEAD_DIM
    gn = gn_ref[0]
    slot_rows = [slice(slot * tq, (slot + 1) * tq) for slot in range(n_slots)]

    def gate(slot, branch):
        v, half = divmod(slot, KV_HEADS)
        c = (half * HPG + v) * N_BRANCH + branch
        return gn[:, c:c + 1]

    q = q_ref[0].astype(F32)
    for slot, rs in enumerate(slot_rows):
        v, half = divmod(slot, KV_HEADS)
        qv = q[:, v * LANES:(v + 1) * LANES]
        qaug_ref[rs, :LANES] = (jnp.where(low, qv, 0.0) if half == 0 else jnp.where(low, 0.0, qv)).astype(BF16)

    cmp_ok = lane * CMP_STRIDE + (CMP_BLK - 1) <= t_pos
    s_c = _dot(qaug_ref[:, :LANES], kct_ref[0])
    p_c = jnp.concatenate(
        [jnp.where(cmp_ok, _softmax_rows(jnp.where(cmp_ok, s_c[rs], NEG)), 0.0) for rs in slot_rows], axis=0)
    o_c = _dot(p_c.astype(BF16), vc_ref[0])
    for slot, rs in enumerate(slot_rows):
        out_ref[rs, :] = gate(slot, 0) * o_c[rs]

    blk = lax.broadcasted_iota(jnp.int32, (n_sel, tq), 0)
    cur = _div(start + lax.broadcasted_iota(jnp.int32, (n_sel, tq), 1), SEL_BLK)
    valid = blk <= cur
    forced = valid & ((blk == 0) | (blk == cur) | (blk == cur - 1))
    for half in range(KV_HEADS):
        psum = p_c[slot_rows[half]]
        for v in range(1, HPG):
            psum = psum + p_c[slot_rows[v * KV_HEADS + half]]
        hi, lo = _split_hi_lo(psum)
        imp = (_dot_nt(c2st_ref[...], hi) + _dot_nt(c2st_ref[...], lo))[:n_sel]
        score = jnp.where(forced, BIG, jnp.where(valid, imp, -BIG))
        rank = jnp.zeros((n_sel, tq), F32)
        for i in range(n_sel):
            r = score[i:i + 1, :]
            tie = jnp.where(blk > i, 1.0, 0.0)
            rank = rank + jnp.where(r > score, 1.0, jnp.where(r == score, tie, 0.0))
        pen_t = jnp.concatenate([jnp.where(rank >= TOP_N, NEG, 0.0), jnp.zeros((LANES - n_sel, tq), F32)], axis=0)
        pen = pen_t.T.astype(BF16)
        for v in range(HPG):
            qaug_ref[slot_rows[v * KV_HEADS + half], LANES:] = pen

    def reset():
        m_ref[...] = jnp.full((rows, LANES), NEG, F32)
        l_ref[...] = jnp.zeros((rows, LANES), F32)
        acc_ref[...] = jnp.zeros((rows, LANES), F32)

    d = lax.broadcasted_iota(jnp.int32, (tq, kt), 1) - lax.broadcasted_iota(jnp.int32, (tq, kt), 0)
    visible = {"causal": d <= 0, "lower": d > 0}

    def tile_step(kv_ref, koff, select, masks, only_step=False):
        w = len(masks) * kt
        kv = kv_ref[0, :, pl.ds(koff, w)]
        if select:
            s_all = _dot(qaug_ref[...], jnp.concatenate([kv[:KV_LANES], e_ref[:, pl.ds(koff, w)]], axis=0))
        else:
            s_all = _dot(qaug_ref[:, :LANES], kv[:KV_LANES])
        for rs in slot_rows:
            s = jnp.concatenate(
                [s_all[rs, jb * kt:(jb + 1) * kt] if mask is None
                 else jnp.where(visible[mask], s_all[rs, jb * kt:(jb + 1) * kt], NEG)
                 for jb, mask in enumerate(masks)], axis=1)
            if only_step:
                m_new = jnp.broadcast_to(jnp.max(s, axis=-1, keepdims=True), (tq, LANES))
            else:
                m_prev = m_ref[rs, :]
                m_new = jnp.maximum(m_prev, jnp.max(s, axis=-1, keepdims=True))
                alpha_ref[rs, :] = jnp.exp(m_prev - m_new)
                m_ref[rs, :] = m_new
            p_ref[rs, :w] = jnp.exp(s - jnp.concatenate([m_new] * (w // LANES), axis=1)).astype(BF16)
        pv = _dot_nt(p_ref[:, :w], jnp.concatenate([kv[KV_LANES:], jnp.ones((KV_LANES, w), BF16)], axis=0))
        if only_step:
            return pv
        acc_ref[...] = alpha_ref[...] * acc_ref[...] + pv[:, :LANES]
        l_ref[...] = alpha_ref[...] * l_ref[...] + pv[:, LANES:]

    def add_branch(branch, acc, den):
        for slot, rs in enumerate(slot_rows):
            out_ref[rs, :] = out_ref[rs, :] + gate(slot, branch) * (acc[rs, :] / den[rs, :])

    reset()
    n_full = start // kt

    def sel_body(j, carry):
        tile_step(ks_ref, pl.multiple_of(j * 2 * kt, 2 * kt), True, (None, None))
        return carry

    lax.fori_loop(0, n_full // 2, sel_body, 0)

    @pl.when(n_full % 2 == 1)
    def _():
        tile_step(ks_ref, pl.multiple_of(start - kt, kt), True, (None, "causal"))

    @pl.when(n_full % 2 == 0)
    def _():
        tile_step(ks_ref, pl.multiple_of(start, kt), True, ("causal",))
    add_branch(1, acc_ref, l_ref)

    n_win = WINDOW // kt
    for n_before in range(n_win + 1):
        masks = [None] * n_before + ["causal"]
        if n_before == n_win:
            masks[0] = "lower"

        @pl.when(start >= WINDOW if n_before == n_win else start == n_before * kt)
        def _():
            pv = tile_step(kw_ref, pl.multiple_of(start - n_before * kt, kt), False, masks, only_step=True)
            add_branch(2, pv[:, :LANES], pv[:, LANES:])

    o_ref[0] = jnp.concatenate(
        [jnp.where(low, out_ref[slot_rows[2 * v], :], out_ref[slot_rows[2 * v + 1], :]) for v in range(HPG)],
        axis=1).astype(BF16)


def _nsa_prompt(q, gn, kct, vc, kst, kwt, e_t, c2s_t):
    b, t, _ = q.shape
    tq = kt = KEY_TILE
    assert WINDOW % kt == 0 and t % kt == 0
    n_cmp = vc.shape[1]
    rows = N_HEADS * tq
    tile = lambda n: pl.BlockSpec((1, tq, n), lambda i, j: (i, j, 0))
    per_b = lambda r, n: pl.BlockSpec((1, r, n), lambda i, j: (i, 0, 0))
    rows_f32 = pltpu.VMEM((rows, LANES), F32)
    return pl.pallas_call(
        functools.partial(_nsa_prompt_body, tq=tq, kt=kt, t_len=t),
        grid=(b, t // tq),
        in_specs=[tile(D_ATTN), tile(LANES), per_b(LANES, n_cmp), per_b(n_cmp, LANES), per_b(KV_ROW, t),
                  per_b(KV_ROW, t), _const_spec(e_t.shape), _const_spec(c2s_t.shape)],
        out_specs=tile(D_ATTN),
        out_shape=jax.ShapeDtypeStruct((b, t, D_ATTN), BF16),
        scratch_shapes=[pltpu.VMEM((rows, 2 * LANES), BF16), pltpu.VMEM((rows, max(2 * kt, WINDOW + kt)), BF16), rows_f32, rows_f32,
                        rows_f32, rows_f32, rows_f32],
        compiler_params=_cparams(2),
        name="nsa_prompt",
    )(q, gn, kct, vc, kst, kwt, e_t, c2s_t)


def _merge_body(x_ref, g_ref, u_ref, um1_ref, um2_ref, o_ref, wb_ref, cw_ref, wco_ref, wao_ref, wmo_ref,
                y_ref, *, tm, tiles_per_seq):
    x = x_ref[...]
    h = _rms(x, g_ref[...]).astype(BF16)
    ub = _dot(h, wb_ref[:, :D_CONV])
    u = u_ref[...]
    if tiles_per_seq is None:
        um1 = um1_ref[...]
        um2 = um2_ref[...]
    else:
        row = lax.broadcasted_iota(jnp.int32, (tm, D_CONV), 0)
        keep = jnp.where(pl.program_id(0) % tiles_per_seq == 0, 0.0, 1.0)
        prev = um1_ref[...] * keep
        um1 = jnp.where(row == 0, prev[7:8, :], pltpu.roll(u, 1, axis=0))
        um2 = jnp.where(row == 0, prev[6:7, :], jnp.where(row == 1, prev[7:8, :], pltpu.roll(u, 2, axis=0)))
    conv = cw_ref[0:1, :] * um2 + cw_ref[1:2, :] * um1 + cw_ref[2:3, :] * u
    y_c = _dot((ub * conv).astype(BF16), wco_ref[...])
    y_a = _dot(o_ref[...], wao_ref[...])
    g_c = jax.nn.sigmoid(_dot(h, wb_ref[:, D_CONV:D_CONV + D_MODEL]))
    g_a = jax.nn.sigmoid(_dot(h, wb_ref[:, D_CONV + D_MODEL:]))
    y_ref[...] = x + _dot((g_c * y_c + g_a * y_a).astype(BF16), wmo_ref[...])


def _merge(x, g, uin, um1, um2, o, wb, cw, wco, wao, wmo, *, seq_len):
    m = x.shape[0]
    tm = min(2 * ROW_TILE, m)
    row = lambda n: pl.BlockSpec((tm, n), lambda i: (i, 0))
    if seq_len is None:
        prev_specs = [row(D_CONV), row(D_CONV)]
        tiles_per_seq = None
    else:
        halo = pl.BlockSpec((8, D_CONV), lambda i: (jnp.maximum(i * (tm // 8) - 1, 0), 0))
        prev_specs = [halo, halo]
        tiles_per_seq = seq_len // tm
    return pl.pallas_call(
        functools.partial(_merge_body, tm=tm, tiles_per_seq=tiles_per_seq),
        grid=(m // tm,),
        in_specs=[row(D_MODEL), _const_spec((1, D_MODEL)), row(D_CONV)] + prev_specs +
                 [row(D_ATTN), _const_spec(wb.shape), _const_spec(cw.shape), _const_spec(wco.shape),
                  _const_spec(wao.shape), _const_spec(wmo.shape)],
        out_specs=row(D_MODEL),
        out_shape=jax.ShapeDtypeStruct((m, D_MODEL), F32),
        compiler_params=_cparams(1),
        name="merge",
    )(x, g, uin, um1, um2, o, wb, cw, wco, wao, wmo)


def _expand_q(q_ref):
    row = lax.broadcasted_iota(jnp.int32, (N_HEADS, LANES), 0)
    lane = lax.broadcasted_iota(jnp.int32, (N_HEADS, LANES), 1)
    q = q_ref[0].astype(F32)
    qe = jnp.zeros((N_HEADS, LANES), F32)
    for v in range(HPG):
        qv = jnp.broadcast_to(q[:, v * LANES:(v + 1) * LANES], (N_HEADS, LANES))
        qe = jnp.where((_div(row, KV_HEADS) == v) & (_div(lane, HEAD_DIM) == _mod(row, KV_HEADS)), qv, qe)
    return qe


def _sample_cmp_body(pt_ref, cache_ref, q_ref, pef_ref, w1_ref, w2k_ref, w2v_ref, c2s_ref,
                     ocmp_ref, idx_ref, buf_ref, rows_ref, sem_ref, *, n_pages, page, t_pos):
    b = pl.program_id(0)
    n_b = pl.num_programs(0)
    n_chunks = n_pages * page // CMP_STRIDE
    n_sel = t_pos // SEL_BLK + 1

    def page_copy(bb, p, slot, wait=False):
        src = 0 if wait else pt_ref[bb * n_pages + p]
        return pltpu.make_async_copy(cache_ref.at[src], buf_ref.at[slot, p], sem_ref.at[slot])

    def start_all(bb, slot):
        for p in range(n_pages):
            page_copy(bb, p, slot).start()

    @pl.when(b == 0)
    def _():
        start_all(0, 0)

    @pl.when(b + 1 < n_b)
    def _():
        start_all(b + 1, (b + 1) % 2)

    slot = b % 2
    for p in range(n_pages):
        page_copy(b, p, slot, wait=True).wait()

    group = 8
    for p0 in range(0, n_pages, group):
        for kv in range(2):
            wide = jnp.concatenate([buf_ref[slot, p0 + i, kv] for i in range(group)], axis=1)
            _store_chunks(rows_ref, kv, p0 * page // CMP_STRIDE, wide.T)
    kc, vc = _compress(rows_ref, n_chunks, pef_ref, w1_ref, w2k_ref, w2v_ref)

    lane = lax.broadcasted_iota(jnp.int32, (N_HEADS, LANES), 1)
    qe = _expand_q(q_ref)
    cidx = lax.broadcasted_iota(jnp.int32, (N_HEADS, n_chunks), 1)
    cmp_ok = cidx * CMP_STRIDE + (CMP_BLK - 1) <= t_pos
    p = jnp.where(cmp_ok, _softmax_rows(jnp.where(cmp_ok, _dot_nt(qe.astype(BF16), kc.astype(BF16)), NEG)), 0.0)
    ocmp_ref[0] = _dot(p.astype(BF16), vc.astype(BF16))

    hi, lo = _split_hi_lo(p)
    imp_h = _dot(hi, c2s_ref[...]) + _dot(lo, c2s_ref[...])
    n_lanes = c2s_ref.shape[1]
    grow = lax.broadcasted_iota(jnp.int32, (N_HEADS, n_lanes), 0)
    glane = lax.broadcasted_iota(jnp.int32, (N_HEADS, n_lanes), 1)
    imp = jnp.zeros((N_HEADS, n_lanes), F32)
    for g in range(KV_HEADS):
        tot = jnp.sum(jnp.where(_mod(grow, KV_HEADS) == g, imp_h, 0.0), axis=0, keepdims=True)
        imp = jnp.where(grow == g, jnp.broadcast_to(tot, imp.shape), imp)
    cur = t_pos // SEL_BLK
    forced = (glane == 0) | (glane == cur) | (glane == cur - 1)
    score = jnp.where(glane < n_sel, jnp.where(forced, BIG, imp), GONE)
    rank = jnp.zeros((N_HEADS, n_lanes), F32)
    for i in range(n_sel):
        col = score[:, i:i + 1]
        tie = jnp.where(glane > i, 1.0, 0.0)
        rank = rank + jnp.where(col > score, 1.0, jnp.where(col == score, tie, 0.0))
    flane = glane.astype(F32)
    picked = jnp.zeros((N_HEADS, LANES), jnp.int32)
    for k in range(TOP_N):
        block_k = jnp.sum(jnp.where(rank == float(k), flane, 0.0), axis=-1, keepdims=True)
        picked = jnp.where(lane == k, block_k.astype(jnp.int32), picked)
    idx_ref[0] = picked


def _sample_cmp(page_table, cache, q, cw, c2s, *, t_pos):
    bsz, n_pages = page_table.shape
    page = cache.shape[-1]
    assert page == LANES
    gs = pltpu.PrefetchScalarGridSpec(
        num_scalar_prefetch=1,
        grid=(bsz,),
        in_specs=[pl.BlockSpec(memory_space=pl.ANY), pl.BlockSpec((1, 1, D_ATTN), lambda i, pt: (i, 0, 0))] +
                 [pl.BlockSpec(w.shape, (lambda nd: lambda i, pt: (0,) * nd)(w.ndim), pipeline_mode=pl.Buffered(1))
                  for w in (*cw, c2s)],
        out_specs=[pl.BlockSpec((1, N_HEADS, LANES), lambda i, pt: (i, 0, 0))] * 2,
        scratch_shapes=[pltpu.VMEM((2, n_pages, 2, KV_LANES, page), F32),
                        pltpu.VMEM((2, n_pages * page // CMP_STRIDE * CHUNK_PITCH, KV_LANES), F32),
                        pltpu.SemaphoreType.DMA((2,))],
    )
    return pl.pallas_call(
        functools.partial(_sample_cmp_body, n_pages=n_pages, page=page, t_pos=t_pos),
        grid_spec=gs,
        out_shape=[jax.ShapeDtypeStruct((bsz, N_HEADS, LANES), F32),
                   jax.ShapeDtypeStruct((bsz, N_HEADS, LANES), jnp.int32)],
        compiler_params=_cparams(1),
        name="sample_cmp",
    )(page_table.reshape(-1), cache, q, *cw, c2s)


def _sample_attn_body(pt_ref, idx_ref, cache_ref, q_ref, gn_ref, ns_ref, nwt_ref, win_ref, ocmp_ref,
                      o_ref, wout_ref, buf_ref, sem_ref, *, n_pages, page, t_pos):
    b = pl.program_id(0)
    n_b = pl.num_programs(0)
    halves = page // SEL_BLK
    new_blk = t_pos // SEL_BLK
    n_slot = KV_HEADS * TOP_N

    def blk_of(bb, k):
        return idx_ref[bb * n_slot + k]

    def page_copy(bb, k, slot, wait=False):
        j = 0 if wait else jnp.minimum(blk_of(bb, k), new_blk - 1)
        src = 0 if wait else pt_ref[bb * n_pages + _div(j, halves)]
        return pltpu.make_async_copy(cache_ref.at[src], buf_ref.at[slot, k], sem_ref.at[slot])

    def start_all(bb, slot):
        for k in range(n_slot):
            page_copy(bb, k, slot).start()

    @pl.when(b == 0)
    def _():
        start_all(0, 0)

    @pl.when(b + 1 < n_b)
    def _():
        start_all(b + 1, (b + 1) % 2)

    slot = b % 2
    for k in range(n_slot):
        page_copy(b, k, slot, wait=True).wait()

    row = lax.broadcasted_iota(jnp.int32, (N_HEADS, LANES), 0)
    lane = lax.broadcasted_iota(jnp.int32, (N_HEADS, LANES), 1)
    grp0 = _mod(row, KV_HEADS) == 0
    qe = _expand_q(q_ref)
    qb = qe.astype(BF16)

    n_keys = TOP_N * page
    klane = lax.broadcasted_iota(jnp.int32, (N_HEADS, n_keys), 1)
    s_g = []
    for g in range(KV_HEADS):
        kt_g = jnp.concatenate([buf_ref[slot, g * TOP_N + k, 0] for k in range(TOP_N)], axis=1).astype(BF16)
        pen = jnp.full((N_HEADS, n_keys), NEG, F32)
        for k in range(TOP_N):
            j = blk_of(b, g * TOP_N + k)
            lo = k * page + _mod(j, halves) * SEL_BLK
            seen = jnp.where(j >= new_blk, NEG, 0.0)
            pen = jnp.where((klane >= lo) & (klane < lo + SEL_BLK), seen, pen)
        s_g.append(_dot(qb, kt_g) + pen)
    grp0_k = _mod(lax.broadcasted_iota(jnp.int32, (N_HEADS, n_keys), 0), KV_HEADS) == 0
    s = jnp.where(grp0_k, s_g[0], s_g[1])
    r_new = ns_ref[0].astype(BF16).astype(F32)
    k_new, v_new = r_new[:, :KV_LANES], r_new[:, KV_LANES:]
    s_new = jnp.sum(qe * k_new, axis=-1, keepdims=True)
    m = jnp.maximum(jnp.max(s, axis=-1, keepdims=True), s_new)
    p = jnp.exp(s - m)
    p_new = jnp.exp(s_new - m)
    den = jnp.sum(p, axis=-1, keepdims=True) + p_new
    pb = p.astype(BF16)
    pv = []
    for g in range(KV_HEADS):
        vt_g = jnp.concatenate([buf_ref[slot, g * TOP_N + k, 1] for k in range(TOP_N)], axis=1).astype(BF16)
        pv.append(_dot_nt(pb, vt_g))
    o_sel = (jnp.where(grp0, pv[0], pv[1]) + p_new.astype(BF16).astype(F32) * v_new) / den

    w_len = win_ref.shape[2]
    nw = nwt_ref[0]
    pick = lax.broadcasted_iota(jnp.int32, nw.shape, 1) == b
    new_col = jnp.sum(jnp.where(pick, nw, 0.0), axis=-1, keepdims=True)
    wlane = lax.broadcasted_iota(jnp.int32, (KV_ROW, w_len), 1)
    wout = jnp.where(wlane == w_len - 1, new_col, pltpu.roll(win_ref[0], w_len - 1, axis=1))
    wout_ref[0] = wout
    sw = _dot(qb, wout[:KV_LANES].astype(BF16))
    ew = jnp.exp(sw - jnp.max(sw, axis=-1, keepdims=True))
    o_win = _dot_nt(ew.astype(BF16), wout[KV_LANES:].astype(BF16)) / jnp.sum(ew, axis=-1, keepdims=True)

    gn = jnp.broadcast_to(gn_ref[0], (N_HEADS, LANES))
    head = _mod(row, KV_HEADS) * HPG + _div(row, KV_HEADS)

    def gate(branch):
        return jnp.sum(jnp.where(lane == head * N_BRANCH + branch, gn, 0.0), axis=-1, keepdims=True)

    out = gate(0) * ocmp_ref[0] + gate(1) * o_sel + gate(2) * o_win
    o_ref[0] = jnp.concatenate(
        [jnp.sum(jnp.where((_div(row, KV_HEADS) == v) & (_div(lane, HEAD_DIM) == _mod(row, KV_HEADS)), out, 0.0),
                 axis=0, keepdims=True)
         for v in range(HPG)], axis=1).astype(BF16)


def _sample_attn(page_table, idx, cache, q, gn, new_sel, new_win_t, win_t, ocmp, *, t_pos):
    bsz, n_pages = page_table.shape
    page = cache.shape[-1]
    w_len = win_t.shape[2]
    one = lambda n: pl.BlockSpec((1, 1, n), lambda i, *_: (i, 0, 0))
    win_spec = pl.BlockSpec((1, KV_ROW, w_len), lambda i, *_: (i, 0, 0))
    gs = pltpu.PrefetchScalarGridSpec(
        num_scalar_prefetch=2,
        grid=(bsz,),
        in_specs=[pl.BlockSpec(memory_space=pl.ANY), one(D_ATTN), one(LANES), one(KV_ROW),
                  pl.BlockSpec(new_win_t.shape, lambda i, *_: (0, 0, 0), pipeline_mode=pl.Buffered(1)), win_spec,
                  pl.BlockSpec((1, N_HEADS, LANES), lambda i, *_: (i, 0, 0))],
        out_specs=[one(D_ATTN), win_spec],
        scratch_shapes=[pltpu.VMEM((2, KV_HEADS * TOP_N, 2, KV_LANES, page), F32), pltpu.SemaphoreType.DMA((2,))],
    )
    return pl.pallas_call(
        functools.partial(_sample_attn_body, n_pages=n_pages, page=page, t_pos=t_pos),
        grid_spec=gs,
        out_shape=[jax.ShapeDtypeStruct((bsz, 1, D_ATTN), BF16), jax.ShapeDtypeStruct(win_t.shape, F32)],
        compiler_params=_cparams(1),
        name="sample_attn",
    )(page_table.reshape(-1), idx.reshape(-1), cache, q, gn, new_sel, new_win_t, win_t, ocmp)


def _rope_tables(pos):
    half = HEAD_DIM // 2
    freqs = jnp.power(ROPE_THETA, -jnp.arange(half, dtype=F32) * (2.0 / HEAD_DIM))
    ang = pos.astype(F32)[:, None] * freqs[None, :]
    cos, sin = jnp.cos(ang), jnp.sin(ang)
    reps = LANES // HEAD_DIM
    return (jnp.tile(jnp.concatenate([cos, cos], axis=1), (1, reps)),
            jnp.tile(jnp.concatenate([-sin, sin], axis=1), (1, reps)), cos.T, sin.T)


def _compress_weights(w1k, w2k, pek, w1v, w2v, pev):
    flat = CMP_STRIDE * KV_LANES

    def pe_rows(pe):
        both = jnp.broadcast_to(pe.reshape(CMP_RATIO, CMP_STRIDE, 1, HEAD_DIM),
                                (CMP_RATIO, CMP_STRIDE, KV_HEADS, HEAD_DIM))
        return jnp.pad(both.reshape(CMP_RATIO, flat), ((0, 8 - CMP_RATIO), (0, 0)))

    def first(w1):
        base = w1.transpose(1, 2, 0, 3)
        full = jnp.zeros((CMP_STRIDE, KV_HEADS, HEAD_DIM, KV_HEADS, CMP_RATIO, CMP_HID), w1.dtype)
        for g in range(KV_HEADS):
            full = full.at[:, g, :, g].set(base)
        return full.reshape(flat, KV_HEADS * CMP_RATIO * CMP_HID)

    def pair(w2):
        z = jnp.zeros_like(w2)
        return jnp.concatenate([jnp.concatenate([w2, z], axis=1), jnp.concatenate([z, w2], axis=1)],
                               axis=0).astype(BF16)

    return (jnp.stack([pe_rows(pek), pe_rows(pev)]).astype(BF16), jnp.stack([first(w1k), first(w1v)]).astype(BF16),
            pair(w2k), pair(w2v))


def _cmp_to_sel(n_cmp_pad, n_sel_pad, n_cmp, n_sel):
    i = np.arange(n_cmp_pad)[:, None]
    j = np.arange(n_sel_pad)[None, :]
    start = i * CMP_STRIDE
    return (start < j * SEL_BLK + SEL_BLK) & (start + CMP_BLK > j * SEL_BLK) & (i < n_cmp) & (j < n_sel)


def _kv_out(rows_t):
    n, _, t = rows_t.shape
    return rows_t.reshape(1, n, 2, KV_HEADS, HEAD_DIM, t).transpose(0, 1, 5, 2, 3, 4)


def _kv_in(a):
    n, t = a.shape[:2]
    return a.transpose(0, 2, 3, 4, 1).reshape(n, 2, KV_LANES, t)


def kernel(x_prompt, x_sample, cache_cmp_kv, cache_sel_kv, state_win_kv, state_conv, page_table, norm_ffn1, ffn1_w_gate, ffn1_w_up, ffn1_w_down, norm_mix, w_mix_in, conv_w, w_cmpk1, w_cmpk2, pe_cmpk, w_cmpv1, w_cmpv2, pe_cmpv, w_conv_out, w_attn_out, w_mix_out, norm_ffn2, ffn2_w_gate, ffn2_w_up, ffn2_w_down, norm_final):
    depth = norm_ffn1.shape[0]
    assert depth == 1
    bsz, t_len, _ = x_prompt.shape
    dbsz, dseq, _ = x_sample.shape
    n_pages = page_table.shape[1]
    page = cache_cmp_kv.shape[2]
    past = n_pages * page
    w_len = state_win_kv.shape[2]
    assert dseq == 1 and w_len == WINDOW and past >= WINDOW and past % SEL_BLK == 0 and page % SEL_BLK == 0
    assert t_len >= WINDOW and t_len // CMP_STRIDE == LANES

    offs = np.cumsum(IN_SPLITS)[:-1].tolist()
    w_ub, w_uc, w_ux, w_q, w_kvc, w_kvs, w_kvw, w_gn, w_gb = jnp.split(w_mix_in[0], offs, axis=1)
    order = np.asarray(HEAD_ORDER)
    w_q = w_q.reshape(D_MODEL, N_HEADS, HEAD_DIM)[:, order].reshape(D_MODEL, D_ATTN)
    w_gn = jnp.pad(w_gn, ((0, 0), (0, LANES - w_gn.shape[1])))
    w_r = jnp.concatenate([w_uc, w_ux, w_q, w_gn], axis=1).astype(BF16)
    w_r_s = jnp.concatenate([w_uc, w_ux, w_q, w_gn, w_kvs, w_kvw], axis=1).astype(BF16)
    w_t = jnp.concatenate([w_kvc, w_kvs, w_kvw], axis=1).T.astype(BF16)
    w_b = jnp.concatenate([w_ub, w_gb], axis=1).astype(BF16)
    w_ao = w_attn_out[0].reshape(N_HEADS, HEAD_DIM, D_MODEL)[order].reshape(D_ATTN, D_MODEL).astype(BF16)
    w_co = w_conv_out[0].astype(BF16)
    w_mo = w_mix_out[0].astype(BF16)
    cw = _compress_weights(w_cmpk1[0], w_cmpk2[0], pe_cmpk[0], w_cmpv1[0], w_cmpv2[0], pe_cmpv[0])
    ffn1 = (norm_ffn1, ffn1_w_gate[0].astype(BF16), ffn1_w_up[0].astype(BF16), ffn1_w_down[0].astype(BF16))
    ffn2 = (norm_ffn2, ffn2_w_gate[0].astype(BF16), ffn2_w_up[0].astype(BF16), ffn2_w_down[0].astype(BF16))
    g_final = norm_final.reshape(1, D_MODEL)

    xp = _ffn(x_prompt.reshape(bsz * t_len, D_MODEL), *ffn1, g_final, final_norm=False)
    tabs = _rope_tables(jnp.arange(t_len, dtype=jnp.int32))
    uin, q, gn, rct, rst, rwt, kst, kwt = _mixin(xp.reshape(bsz, t_len, D_MODEL), norm_mix, w_r, w_t, tabs,
                                                 rows_out=False)
    kct, vc = _compress_prompt(rct, cw)
    n_chunks = t_len // CMP_STRIDE
    n_sel = t_len // SEL_BLK
    e_t = jnp.asarray(np.arange(LANES)[:, None] == np.arange(t_len)[None, :] // SEL_BLK, BF16)
    c2s_t = jnp.asarray(_cmp_to_sel(n_chunks, LANES, n_chunks - CMP_RATIO + 1, n_sel).T, BF16)
    o = _nsa_prompt(q, gn, kct, vc, kst, kwt, e_t, c2s_t)
    uin = uin.reshape(bsz * t_len, D_CONV)
    xp = _merge(xp, norm_mix, uin, uin, uin, o.reshape(bsz * t_len, D_ATTN), w_b, conv_w[0], w_co, w_ao, w_mo,
                seq_len=t_len)
    y_prompt = _ffn(xp, *ffn2, g_final, final_norm=True).reshape(bsz, t_len, D_MODEL)
    cmp_p = _kv_out(rct)
    sel_p = _kv_out(rst)
    win_p = _kv_out(rwt[:, :, t_len - WINDOW:])
    conv_p = uin.reshape(1, bsz, t_len, D_CONV)[:, :, t_len - (CONV_W - 1):]

    xs = _ffn(x_sample.reshape(dbsz, D_MODEL), *ffn1, g_final, final_norm=False)
    tabs_s = _rope_tables(jnp.full((dbsz,), past, jnp.int32))
    uin_s, q_s, gn_s, rct_s, rst_s, rwt_s, _, _, rs_s, _ = _mixin(xs.reshape(1, dbsz, D_MODEL), norm_mix, w_r_s,
                                                                  w_t, tabs_s, rows_out=True)
    one = lambda a: a.reshape(dbsz, 1, a.shape[-1])
    n_chunks_s = past // CMP_STRIDE
    n_sel_s = past // SEL_BLK + 1
    n_sel_pad = -(-n_sel_s // LANES) * LANES
    c2s_s = jnp.asarray(_cmp_to_sel(n_chunks_s, n_sel_pad, (past + 1) // CMP_STRIDE - CMP_RATIO + 1, n_sel_s), BF16)
    ocmp, idx = _sample_cmp(page_table, _kv_in(cache_cmp_kv[0]), one(q_s), cw, c2s_s, t_pos=past)
    idx = idx[:, :KV_HEADS, :TOP_N]
    win_t = _kv_in(state_win_kv[0]).reshape(dbsz, KV_ROW, w_len)
    o_s, win_s = _sample_attn(page_table, idx, _kv_in(cache_sel_kv[0]), one(q_s), one(gn_s), one(rs_s), rwt_s,
                              win_t, ocmp, t_pos=past)
    uin_s = uin_s.reshape(dbsz, D_CONV)
    xs = _merge(xs, norm_mix, uin_s, state_conv[0, :, 1], state_conv[0, :, 0], o_s.reshape(dbsz, D_ATTN), w_b,
                conv_w[0], w_co, w_ao, w_mo, seq_len=None)
    y_sample = _ffn(xs, *ffn2, g_final, final_norm=True).reshape(dbsz, 1, D_MODEL)
    new_out = lambda rt: rt.reshape(1, 2, KV_HEADS, HEAD_DIM, dbsz).transpose(0, 4, 1, 2, 3)[:, :, None]
    cmp_s = new_out(rct_s)
    sel_s = new_out(rst_s)
    win_s = _kv_out(win_s)
    conv_s = jnp.concatenate([state_conv[0, :, 1:], uin_s[:, None, :]], axis=1)[None]
    return (y_prompt, y_sample, cmp_p, sel_p, win_p, conv_p, cmp_s, sel_s, win_s, conv_s)
```

```python
import functools

import numpy as np
import jax
import jax.numpy as jnp
from jax import lax
from jax.experimental import pallas as pl
from jax.experimental.pallas import tpu as pltpu

F32 = jnp.float32
BF16 = jnp.bfloat16

D_MODEL = 1024
D_CONV = 512
CONV_W = 3
N_HEADS = 8
KV_HEADS = 2
HPG = N_HEADS // KV_HEADS
HEAD_DIM = 64
D_ATTN = N_HEADS * HEAD_DIM
KV_ROW = 2 * KV_HEADS * HEAD_DIM
CMP_STRIDE = 16
CMP_RATIO = 2
CMP_BLK = CMP_STRIDE * CMP_RATIO
CMP_HID = 128
SEL_BLK = 64
TOP_N = 16
WINDOW = 512
N_BRANCH = 3
D_FF = 2816
ROPE_THETA = 10000.0
EPS = 1e-6
NEG = -1e30
BIG = 1e9
GONE = -3e38
IN_SPLITS = (D_CONV, D_CONV, D_CONV, D_ATTN, KV_ROW, KV_ROW, KV_ROW, N_HEADS * N_BRANCH, 2 * D_MODEL)

LANES = 128
KV_LANES = KV_HEADS * HEAD_DIM
assert KV_LANES == LANES and KV_HEADS == 2 and 2 * HEAD_DIM == LANES
HEAD_ORDER = tuple(half * HPG + v for v in range(HPG) for half in range(KV_HEADS))
VMEM_LIMIT = 56 * 1024 * 1024
ROW_TILE = 512
KEY_TILE = 256


def _cparams(n_grid):
    return pltpu.CompilerParams(dimension_semantics=("arbitrary",) * n_grid, vmem_limit_bytes=VMEM_LIMIT)


def _const_spec(shape):
    nd = len(shape)
    return pl.BlockSpec(shape, lambda *_: (0,) * nd, pipeline_mode=pl.Buffered(1))


def _rms(x, g):
    return x * lax.rsqrt(jnp.mean(x * x, axis=-1, keepdims=True) + EPS) * g


def _silu(x):
    return x * jax.nn.sigmoid(x)


def _div(x, n):
    return lax.shift_right_logical(x, int(np.log2(n)))


def _mod(x, n):
    return x & (n - 1)


def _dot(a, b):
    return jnp.dot(a, b, preferred_element_type=F32)


def _dot_nt(a, b):
    return lax.dot_general(a, b, (((1,), (1,)), ((), ())), preferred_element_type=F32)


def _ffn_body(x_ref, g_ref, wg_ref, wu_ref, wd_ref, gf_ref, o_ref, *, final_norm):
    x = x_ref[...]
    h = _rms(x, g_ref[...]).astype(BF16)
    a = (_silu(_dot(h, wg_ref[...])) * _dot(h, wu_ref[...])).astype(BF16)
    y = x + 0.5 * _dot(a, wd_ref[...])
    if final_norm:
        y = _rms(y, gf_ref[...])
    o_ref[...] = y


def _ffn(x, g, wg, wu, wd, gf, *, final_norm):
    m = x.shape[0]
    tm = min(ROW_TILE, m)
    row = pl.BlockSpec((tm, D_MODEL), lambda i: (i, 0))
    return pl.pallas_call(
        functools.partial(_ffn_body, final_norm=final_norm),
        grid=(m // tm,),
        in_specs=[row, _const_spec((1, D_MODEL)), _const_spec(wg.shape), _const_spec(wu.shape),
                  _const_spec(wd.shape), _const_spec((1, D_MODEL))],
        out_specs=row,
        out_shape=jax.ShapeDtypeStruct((m, D_MODEL), F32),
        compiler_params=_cparams(1),
        name="ffn",
    )(x, g, wg, wu, wd, gf)


ROW_COLS = {"uc": (0, 512), "ux": (512, 1024), "q": (1024, 1536), "gn": (1536, 1664), "kvs": (1664, 1920),
            "kvw": (1920, 2176)}


def _rope_tile(z, cos, sin):
    first = _mod(lax.broadcasted_iota(jnp.int32, z.shape, 1), HEAD_DIM) < (HEAD_DIM // 2)
    rot = jnp.where(first, pltpu.roll(z, LANES - HEAD_DIM // 2, axis=1), pltpu.roll(z, HEAD_DIM // 2, axis=1))
    return z * cos + rot * sin


def _rope_rows_t(zt, cos_t, sin_t):
    half = HEAD_DIM // 2
    out = []
    for hh in range(KV_LANES // HEAD_DIM):
        x1 = zt[hh * HEAD_DIM:hh * HEAD_DIM + half]
        x2 = zt[hh * HEAD_DIM + half:(hh + 1) * HEAD_DIM]
        out += [x1 * cos_t - x2 * sin_t, x1 * sin_t + x2 * cos_t]
    return jnp.concatenate(out, axis=0)


def _mixin_body(x_ref, g_ref, wr_ref, wt_ref, cos_ref, sin_ref, cost_ref, sint_ref,
                uin_ref, q_ref, gn_ref, rct_ref, rst_ref, rwt_ref, kst_ref, kwt_ref, *row_refs):
    h = _rms(x_ref[0], g_ref[...]).astype(BF16)
    cos = cos_ref[...]
    sin = sin_ref[...]

    def proj(name):
        lo, hi = ROW_COLS[name]
        return _dot(h, wr_ref[:, lo:hi])

    uin_ref[0] = proj("uc") * proj("ux")
    zq = proj("q")
    scale = HEAD_DIM ** -0.5
    q_ref[0] = jnp.concatenate(
        [_rope_tile(zq[:, v * LANES:(v + 1) * LANES], cos, sin) * scale for v in range(D_ATTN // LANES)],
        axis=1).astype(BF16)
    gn_ref[0] = jax.nn.sigmoid(proj("gn"))
    cos_t = cost_ref[...]
    sin_t = sint_ref[...]
    for i, (f_ref, b_ref) in enumerate(((rct_ref, None), (rst_ref, kst_ref), (rwt_ref, kwt_ref))):
        zt = _dot_nt(wt_ref[i * KV_ROW:(i + 1) * KV_ROW, :], h)
        rows_t = jnp.concatenate([_rope_rows_t(zt[:KV_LANES], cos_t, sin_t), zt[KV_LANES:]], axis=0)
        f_ref[0] = rows_t
        if b_ref is not None:
            b_ref[0] = rows_t.astype(BF16)
    for name, r_ref in zip(("kvs", "kvw"), row_refs):
        z = proj(name)
        r_ref[0] = jnp.concatenate([_rope_tile(z[:, :KV_LANES], cos, sin), z[:, KV_LANES:]], axis=1)


def _mixin(x, g, wr, wt, tabs, *, rows_out):
    b, t, _ = x.shape
    tm = min(2 * ROW_TILE, t)
    row = lambda n: pl.BlockSpec((1, tm, n), lambda i, j: (i, j, 0))
    col = pl.BlockSpec((1, KV_ROW, tm), lambda i, j: (i, 0, j))
    tab = pl.BlockSpec((tm, LANES), lambda i, j: (j, 0))
    tab_t = pl.BlockSpec((HEAD_DIM // 2, tm), lambda i, j: (0, j))
    rsd = lambda n, dt: jax.ShapeDtypeStruct((b, t, n), dt)
    csd = lambda dt: jax.ShapeDtypeStruct((b, KV_ROW, t), dt)
    extra = 2 if rows_out else 0
    return pl.pallas_call(
        _mixin_body,
        grid=(b, t // tm),
        in_specs=[row(D_MODEL), _const_spec((1, D_MODEL)), _const_spec(wr.shape), _const_spec(wt.shape),
                  tab, tab, tab_t, tab_t],
        out_specs=[row(D_CONV), row(D_ATTN), row(LANES), col, col, col, col, col] + [row(KV_ROW)] * extra,
        out_shape=[rsd(D_CONV, F32), rsd(D_ATTN, BF16), rsd(LANES, F32), csd(F32), csd(F32), csd(F32), csd(BF16),
                   csd(BF16)] + [rsd(KV_ROW, F32)] * extra,
        compiler_params=_cparams(2),
        name="mix_in",
    )(x, g, wr, wt, *tabs)


CHUNK_PITCH = 24


def _store_chunks(rows_ref, kv, first_chunk, rows):
    for j in range(rows.shape[0] // CMP_STRIDE):
        lo = (first_chunk + j) * CHUNK_PITCH
        rows_ref[kv, lo:lo + CMP_STRIDE, :] = rows[j * CMP_STRIDE:(j + 1) * CMP_STRIDE]


def _compress(rows_ref, n_chunks, pef_ref, w1_ref, w2k_ref, w2v_ref):
    keep = lax.broadcasted_iota(jnp.int32, (n_chunks, LANES), 0) < n_chunks - 1
    out = []
    for kv, w2_ref in enumerate((w2k_ref, w2v_ref)):
        lhs = jnp.concatenate([rows_ref[kv, pl.ds(s, n_chunks, stride=CHUNK_PITCH), :].astype(BF16)
                               for s in range(CMP_STRIDE)], axis=1)
        part = _dot(lhs, w1_ref[kv])
        pe_part = _dot(pef_ref[kv], w1_ref[kv])
        hid = []
        for g in range(KV_HEADS):
            c0, c1 = (CMP_RATIO * g) * CMP_HID, (CMP_RATIO * g + 1) * CMP_HID
            bias = pe_part[0:1, c0:c0 + CMP_HID] + pe_part[1:2, c1:c1 + CMP_HID]
            hid.append(part[:, c0:c0 + CMP_HID] + pltpu.roll(part[:, c1:c1 + CMP_HID], n_chunks - 1, axis=0) + bias)
        act = _silu(jnp.concatenate(hid, axis=1)).astype(BF16)
        out.append(jnp.where(keep, _dot(act, w2_ref[...]), 0.0))
    return out


def _compress_prompt_body(xt_ref, pef_ref, w1_ref, w2k_ref, w2v_ref, kct_ref, vc_ref, rows_ref,
                          *, n_chunks, t_len):
    for kv in range(2):
        for c in range(t_len // LANES):
            _store_chunks(rows_ref, kv, c * (LANES // CMP_STRIDE),
                          xt_ref[0, kv * KV_LANES:(kv + 1) * KV_LANES, c * LANES:(c + 1) * LANES].T)
    kc, vc = _compress(rows_ref, n_chunks, pef_ref, w1_ref, w2k_ref, w2v_ref)
    kct_ref[0] = kc.T.astype(BF16)
    vc_ref[0] = vc.astype(BF16)


def _compress_prompt(rct, cw):
    b, _, t = rct.shape
    n_chunks = t // CMP_STRIDE
    out = pl.BlockSpec((1, n_chunks, LANES), lambda i: (i, 0, 0))
    return pl.pallas_call(
        functools.partial(_compress_prompt_body, n_chunks=n_chunks, t_len=t),
        grid=(b,),
        in_specs=[pl.BlockSpec((1, KV_ROW, t), lambda i: (i, 0, 0))] + [_const_spec(w.shape) for w in cw],
        out_specs=[out, out],
        out_shape=[jax.ShapeDtypeStruct((b, n_chunks, LANES), BF16)] * 2,
        scratch_shapes=[pltpu.VMEM((2, n_chunks * CHUNK_PITCH, LANES), F32)],
        compiler_params=_cparams(1),
        name="compress_prompt",
    )(rct, *cw)


def _softmax_rows(s):
    e = jnp.exp(s - jnp.max(s, axis=-1, keepdims=True))
    return e / jnp.sum(e, axis=-1, keepdims=True)


def _split_hi_lo(x):
    hi = x.astype(BF16)
    return hi, (x - hi.astype(F32)).astype(BF16)


def _nsa_prompt_body(q_ref, gn_ref, kct_ref, vc_ref, ks_ref, kw_ref, e_ref, c2st_ref, o_ref,
                     qaug_ref, p_ref, m_ref, l_ref, acc_ref, alpha_ref, out_ref, *, tq, kt, t_len):
    n_slots = N_HEADS
    rows = n_slots * tq
    n_sel = t_len // SEL_BLK
    start = pl.program_id(1) * tq
    lane = lax.broadcasted_iota(jnp.int32, (tq, LANES), 1)
    t_pos = start + lax.broadcasted_iota(jnp.int32, (tq, LANES), 0)
    low = lane < HEAD_DIM
    gn = gn_ref[0]
    slot_rows = [slice(slot * tq, (slot + 1) * tq) for slot in range(n_slots)]

    def gate(slot, branch):
        v, half = divmod(slot, KV_HEADS)
        c = (half * HPG + v) * N_BRANCH + branch
        return gn[:, c:c + 1]

    q = q_ref[0].astype(F32)
    for slot, rs in enumerate(slot_rows):
        v, half = divmod(slot, KV_HEADS)
        qv = q[:, v * LANES:(v + 1) * LANES]
        qaug_ref[rs, :LANES] = (jnp.where(low, qv, 0.0) if half == 0 else jnp.where(low, 0.0, qv)).astype(BF16)

    cmp_ok = lane * CMP_STRIDE + (CMP_BLK - 1) <= t_pos
    s_c = _dot(qaug_ref[:, :LANES], kct_ref[0])
    p_c = jnp.concatenate(
        [jnp.where(cmp_ok, _softmax_rows(jnp.where(cmp_ok, s_c[rs], NEG)), 0.0) for rs in slot_rows], axis=0)
    o_c = _dot(p_c.astype(BF16), vc_ref[0])
    for slot, rs in enumerate(slot_rows):
        out_ref[rs, :] = gate(slot, 0) * o_c[rs]

    blk = lax.broadcasted_iota(jnp.int32, (n_sel, tq), 0)
    cur = _div(start + lax.broadcasted_iota(jnp.int32, (n_sel, tq), 1), SEL_BLK)
    valid = blk <= cur
    forced = valid & ((blk == 0) | (blk == cur) | (blk == cur - 1))
    for half in range(KV_HEADS):
        psum = p_c[slot_rows[half]]
        for v in range(1, HPG):
            psum = psum + p_c[slot_rows[v * KV_HEADS + half]]
        hi, lo = _split_hi_lo(psum)
        imp = (_dot_nt(c2st_ref[...], hi) + _dot_nt(c2st_ref[...], lo))[:n_sel]
        score = jnp.where(forced, BIG, jnp.where(valid, imp, -BIG))
        rank = jnp.zeros((n_sel, tq), F32)
        for i in range(n_sel):
            r = score[i:i + 1, :]
            tie = jnp.where(blk > i, 1.0, 0.0)
            rank = rank + jnp.where(r > score, 1.0, jnp.where(r == score, tie, 0.0))
        pen_t = jnp.concatenate([jnp.where(rank >= TOP_N, NEG, 0.0), jnp.zeros((LANES - n_sel, tq), F32)], axis=0)
        pen = pen_t.T.astype(BF16)
        for v in range(HPG):
            qaug_ref[slot_rows[v * KV_HEADS + half], LANES:] = pen

    def reset():
        m_ref[...] = jnp.full((rows, LANES), NEG, F32)
        l_ref[...] = jnp.zeros((rows, LANES), F32)
        acc_ref[...] = jnp.zeros((rows, LANES), F32)

    d = lax.broadcasted_iota(jnp.int32, (tq, kt), 1) - lax.broadcasted_iota(jnp.int32, (tq, kt), 0)
    visible = {"causal": d <= 0, "lower": d > 0}

    def tile_step(kv_ref, koff, select, masks, only_step=False):
        w = len(masks) * kt
        kv = kv_ref[0, :, pl.ds(koff, w)]
        if select:
            s_all = _dot(qaug_ref[...], jnp.concatenate([kv[:KV_LANES], e_ref[:, pl.ds(koff, w)]], axis=0))
        else:
            s_all = _dot(qaug_ref[:, :LANES], kv[:KV_LANES])
        for rs in slot_rows:
            s = jnp.concatenate(
                [s_all[rs, jb * kt:(jb + 1) * kt] if mask is None
                 else jnp.where(visible[mask], s_all[rs, jb * kt:(jb + 1) * kt], NEG)
                 for jb, mask in enumerate(masks)], axis=1)
            if only_step:
                m_new = jnp.broadcast_to(jnp.max(s, axis=-1, keepdims=True), (tq, LANES))
            else:
                m_prev = m_ref[rs, :]
                m_new = jnp.maximum(m_prev, jnp.max(s, axis=-1, keepdims=True))
                alpha_ref[rs, :] = jnp.exp(m_prev - m_new)
                m_ref[rs, :] = m_new
            p_ref[rs, :w] = jnp.exp(s - jnp.concatenate([m_new] * (w // LANES), axis=1)).astype(BF16)
        pv = _dot_nt(p_ref[:, :w], jnp.concatenate([kv[KV_LANES:], jnp.ones((KV_LANES, w), BF16)], axis=0))
        if only_step:
            return pv
        acc_ref[...] = alpha_ref[...] * acc_ref[...] + pv[:, :LANES]
        l_ref[...] = alpha_ref[...] * l_ref[...] + pv[:, LANES:]

    def add_branch(branch, acc, den):
        for slot, rs in enumerate(slot_rows):
            out_ref[rs, :] = out_ref[rs, :] + gate(slot, branch) * (acc[rs, :] / den[rs, :])

    reset()
    n_full = start // kt

    def sel_body(j, carry):
        tile_step(ks_ref, pl.multiple_of(j * 2 * kt, 2 * kt), True, (None, None))
        return carry

    lax.fori_loop(0, n_full // 2, sel_body, 0)

    @pl.when(n_full % 2 == 1)
    def _():
        tile_step(ks_ref, pl.multiple_of(start - kt, kt), True, (None, "causal"))

    @pl.when(n_full % 2 == 0)
    def _():
        tile_step(ks_ref, pl.multiple_of(start, kt), True, ("causal",))
    add_branch(1, acc_ref, l_ref)

    n_win = WINDOW // kt
    for n_before in range(n_win + 1):
        masks = [None] * n_before + ["causal"]
        if n_before == n_win:
            masks[0] = "lower"

        @pl.when(start >= WINDOW if n_before == n_win else start == n_before * kt)
        def _():
            pv = tile_step(kw_ref, pl.multiple_of(start - n_before * kt, kt), False, masks, only_step=True)
            add_branch(2, pv[:, :LANES], pv[:, LANES:])

    o_ref[0] = jnp.concatenate(
        [jnp.where(low, out_ref[slot_rows[2 * v], :], out_ref[slot_rows[2 * v + 1], :]) for v in range(HPG)],
        axis=1).astype(BF16)


def _nsa_prompt(q, gn, kct, vc, kst, kwt, e_t, c2s_t):
    b, t, _ = q.shape
    tq = kt = KEY_TILE
    assert WINDOW % kt == 0 and t % kt == 0
    n_cmp = vc.shape[1]
    rows = N_HEADS * tq
    tile = lambda n: pl.BlockSpec((1, tq, n), lambda i, j: (i, j, 0))
    per_b = lambda r, n: pl.BlockSpec((1, r, n), lambda i, j: (i, 0, 0))
    rows_f32 = pltpu.VMEM((rows, LANES), F32)
    return pl.pallas_call(
        functools.partial(_nsa_prompt_body, tq=tq, kt=kt, t_len=t),
        grid=(b, t // tq),
        in_specs=[tile(D_ATTN), tile(LANES), per_b(LANES, n_cmp), per_b(n_cmp, LANES), per_b(KV_ROW, t),
                  per_b(KV_ROW, t), _const_spec(e_t.shape), _const_spec(c2s_t.shape)],
        out_specs=tile(D_ATTN),
        out_shape=jax.ShapeDtypeStruct((b, t, D_ATTN), BF16),
        scratch_shapes=[pltpu.VMEM((rows, 2 * LANES), BF16), pltpu.VMEM((rows, max(2 * kt, WINDOW + kt)), BF16), rows_f32, rows_f32,
                        rows_f32, rows_f32, rows_f32],
        compiler_params=_cparams(2),
        name="nsa_prompt",
    )(q, gn, kct, vc, kst, kwt, e_t, c2s_t)


def _merge_body(x_ref, g_ref, u_ref, um1_ref, um2_ref, o_ref, wb_ref, cw_ref, wco_ref, wao_ref, wmo_ref,
                y_ref, *, tm, tiles_per_seq):
    x = x_ref[...]
    h = _rms(x, g_ref[...]).astype(BF16)
    ub = _dot(h, wb_ref[:, :D_CONV])
    u = u_ref[...]
    if tiles_per_seq is None:
        um1 = um1_ref[...]
        um2 = um2_ref[...]
    else:
        row = lax.broadcasted_iota(jnp.int32, (tm, D_CONV), 0)
        keep = jnp.where(pl.program_id(0) % tiles_per_seq == 0, 0.0, 1.0)
        prev = um1_ref[...] * keep
        um1 = jnp.where(row == 0, prev[7:8, :], pltpu.roll(u, 1, axis=0))
        um2 = jnp.where(row == 0, prev[6:7, :], jnp.where(row == 1, prev[7:8, :], pltpu.roll(u, 2, axis=0)))
    conv = cw_ref[0:1, :] * um2 + cw_ref[1:2, :] * um1 + cw_ref[2:3, :] * u
    y_c = _dot((ub * conv).astype(BF16), wco_ref[...])
    y_a = _dot(o_ref[...], wao_ref[...])
    g_c = jax.nn.sigmoid(_dot(h, wb_ref[:, D_CONV:D_CONV + D_MODEL]))
    g_a = jax.nn.sigmoid(_dot(h, wb_ref[:, D_CONV + D_MODEL:]))
    y_ref[...] = x + _dot((g_c * y_c + g_a * y_a).astype(BF16), wmo_ref[...])


def _merge(x, g, uin, um1, um2, o, wb, cw, wco, wao, wmo, *, seq_len):
    m = x.shape[0]
    tm = min(2 * ROW_TILE, m)
    row = lambda n: pl.BlockSpec((tm, n), lambda i: (i, 0))
    if seq_len is None:
        prev_specs = [row(D_CONV), row(D_CONV)]
        tiles_per_seq = None
    else:
        halo = pl.BlockSpec((8, D_CONV), lambda i: (jnp.maximum(i * (tm // 8) - 1, 0), 0))
        prev_specs = [halo, halo]
        tiles_per_seq = seq_len // tm
    return pl.pallas_call(
        functools.partial(_merge_body, tm=tm, tiles_per_seq=tiles_per_seq),
        grid=(m // tm,),
        in_specs=[row(D_MODEL), _const_spec((1, D_MODEL)), row(D_CONV)] + prev_specs +
                 [row(D_ATTN), _const_spec(wb.shape), _const_spec(cw.shape), _const_spec(wco.shape),
                  _const_spec(wao.shape), _const_spec(wmo.shape)],
        out_specs=row(D_MODEL),
        out_shape=jax.ShapeDtypeStruct((m, D_MODEL), F32),
        compiler_params=_cparams(1),
        name="merge",
    )(x, g, uin, um1, um2, o, wb, cw, wco, wao, wmo)


def _expand_q(q_ref):
    row = lax.broadcasted_iota(jnp.int32, (N_HEADS, LANES), 0)
    lane = lax.broadcasted_iota(jnp.int32, (N_HEADS, LANES), 1)
    q = q_ref[0].astype(F32)
    qe = jnp.zeros((N_HEADS, LANES), F32)
    for v in range(HPG):
        qv = jnp.broadcast_to(q[:, v * LANES:(v + 1) * LANES], (N_HEADS, LANES))
        qe = jnp.where((_div(row, KV_HEADS) == v) & (_div(lane, HEAD_DIM) == _mod(row, KV_HEADS)), qv, qe)
    return qe


def _sample_cmp_body(pt_ref, cache_ref, q_ref, pef_ref, w1_ref, w2k_ref, w2v_ref, c2s_ref,
                     ocmp_ref, idx_ref, buf_ref, rows_ref, sem_ref, *, n_pages, page, t_pos):
    b = pl.program_id(0)
    n_b = pl.num_programs(0)
    n_chunks = n_pages * page // CMP_STRIDE
    n_sel = t_pos // SEL_BLK + 1

    def page_copy(bb, p, slot, wait=False):
        src = 0 if wait else pt_ref[bb * n_pages + p]
        return pltpu.make_async_copy(cache_ref.at[src], buf_ref.at[slot, p], sem_ref.at[slot])

    def start_all(bb, slot):
        for p in range(n_pages):
            page_copy(bb, p, slot).start()

    @pl.when(b == 0)
    def _():
        start_all(0, 0)

    @pl.when(b + 1 < n_b)
    def _():
        start_all(b + 1, (b + 1) % 2)

    slot = b % 2
    for p in range(n_pages):
        page_copy(b, p, slot, wait=True).wait()

    group = 8
    for p0 in range(0, n_pages, group):
        for kv in range(2):
            wide = jnp.concatenate([buf_ref[slot, p0 + i, kv] for i in range(group)], axis=1)
            _store_chunks(rows_ref, kv, p0 * page // CMP_STRIDE, wide.T)
    kc, vc = _compress(rows_ref, n_chunks, pef_ref, w1_ref, w2k_ref, w2v_ref)

    lane = lax.broadcasted_iota(jnp.int32, (N_HEADS, LANES), 1)
    qe = _expand_q(q_ref)
    cidx = lax.broadcasted_iota(jnp.int32, (N_HEADS, n_chunks), 1)
    cmp_ok = cidx * CMP_STRIDE + (CMP_BLK - 1) <= t_pos
    p = jnp.where(cmp_ok, _softmax_rows(jnp.where(cmp_ok, _dot_nt(qe.astype(BF16), kc.astype(BF16)), NEG)), 0.0)
    ocmp_ref[0] = _dot(p.astype(BF16), vc.astype(BF16))

    hi, lo = _split_hi_lo(p)
    imp_h = _dot(hi, c2s_ref[...]) + _dot(lo, c2s_ref[...])
    n_lanes = c2s_ref.shape[1]
    grow = lax.broadcasted_iota(jnp.int32, (N_HEADS, n_lanes), 0)
    glane = lax.broadcasted_iota(jnp.int32, (N_HEADS, n_lanes), 1)
    imp = jnp.zeros((N_HEADS, n_lanes), F32)
    for g in range(KV_HEADS):
        tot = jnp.sum(jnp.where(_mod(grow, KV_HEADS) == g, imp_h, 0.0), axis=0, keepdims=True)
        imp = jnp.where(grow == g, jnp.broadcast_to(tot, imp.shape), imp)
    cur = t_pos // SEL_BLK
    forced = (glane == 0) | (glane == cur) | (glane == cur - 1)
    score = jnp.where(glane < n_sel, jnp.where(forced, BIG, imp), GONE)
    rank = jnp.zeros((N_HEADS, n_lanes), F32)
    for i in range(n_sel):
        col = score[:, i:i + 1]
        tie = jnp.where(glane > i, 1.0, 0.0)
        rank = rank + jnp.where(col > score, 1.0, jnp.where(col == score, tie, 0.0))
    flane = glane.astype(F32)
    picked = jnp.zeros((N_HEADS, LANES), jnp.int32)
    for k in range(TOP_N):
        block_k = jnp.sum(jnp.where(rank == float(k), flane, 0.0), axis=-1, keepdims=True)
        picked = jnp.where(lane == k, block_k.astype(jnp.int32), picked)
    idx_ref[0] = picked


def _sample_cmp(page_table, cache, q, cw, c2s, *, t_pos):
    bsz, n_pages = page_table.shape
    page = cache.shape[-1]
    assert page == LANES
    gs = pltpu.PrefetchScalarGridSpec(
        num_scalar_prefetch=1,
        grid=(bsz,),
        in_specs=[pl.BlockSpec(memory_space=pl.ANY), pl.BlockSpec((1, 1, D_ATTN), lambda i, pt: (i, 0, 0))] +
                 [pl.BlockSpec(w.shape, (lambda nd: lambda i, pt: (0,) * nd)(w.ndim), pipeline_mode=pl.Buffered(1))
                  for w in (*cw, c2s)],
        out_specs=[pl.BlockSpec((1, N_HEADS, LANES), lambda i, pt: (i, 0, 0))] * 2,
        scratch_shapes=[pltpu.VMEM((2, n_pages, 2, KV_LANES, page), F32),
                        pltpu.VMEM((2, n_pages * page // CMP_STRIDE * CHUNK_PITCH, KV_LANES), F32),
                        pltpu.SemaphoreType.DMA((2,))],
    )
    return pl.pallas_call(
        functools.partial(_sample_cmp_body, n_pages=n_pages, page=page, t_pos=t_pos),
        grid_spec=gs,
        out_shape=[jax.ShapeDtypeStruct((bsz, N_HEADS, LANES), F32),
                   jax.ShapeDtypeStruct((bsz, N_HEADS, LANES), jnp.int32)],
        compiler_params=_cparams(1),
        name="sample_cmp",
    )(page_table.reshape(-1), cache, q, *cw, c2s)


def _sample_attn_body(pt_ref, idx_ref, cache_ref, q_ref, gn_ref, ns_ref, nwt_ref, win_ref, ocmp_ref,
                      o_ref, wout_ref, buf_ref, sem_ref, *, n_pages, page, t_pos):
    b = pl.program_id(0)
    n_b = pl.num_programs(0)
    halves = page // SEL_BLK
    new_blk = t_pos // SEL_BLK
    n_slot = KV_HEADS * TOP_N

    def blk_of(bb, k):
        return idx_ref[bb * n_slot + k]

    def page_copy(bb, k, slot, wait=False):
        j = 0 if wait else jnp.minimum(blk_of(bb, k), new_blk - 1)
        src = 0 if wait else pt_ref[bb * n_pages + _div(j, halves)]
        feats = pl.ds((k // TOP_N) * HEAD_DIM, HEAD_DIM)
        return pltpu.make_async_copy(cache_ref.at[src, :, feats, :], buf_ref.at[slot, k], sem_ref.at[slot])

    def start_all(bb, slot):
        for k in range(n_slot):
            page_copy(bb, k, slot).start()

    @pl.when(b == 0)
    def _():
        start_all(0, 0)

    @pl.when(b + 1 < n_b)
    def _():
        start_all(b + 1, (b + 1) % 2)

    slot = b % 2
    for k in range(n_slot):
        page_copy(b, k, slot, wait=True).wait()

    row = lax.broadcasted_iota(jnp.int32, (N_HEADS, LANES), 0)
    lane = lax.broadcasted_iota(jnp.int32, (N_HEADS, LANES), 1)
    grp0 = _mod(row, KV_HEADS) == 0
    qe = _expand_q(q_ref)
    qb = qe.astype(BF16)

    n_keys = TOP_N * page
    klane = lax.broadcasted_iota(jnp.int32, (N_HEADS, n_keys), 1)
    s_g = []
    for g in range(KV_HEADS):
        kt_g = jnp.concatenate([buf_ref[slot, g * TOP_N + k, 0] for k in range(TOP_N)], axis=1).astype(BF16)
        pen = jnp.full((N_HEADS, n_keys), NEG, F32)
        for k in range(TOP_N):
            j = blk_of(b, g * TOP_N + k)
            lo = k * page + _mod(j, halves) * SEL_BLK
            seen = jnp.where(j >= new_blk, NEG, 0.0)
            pen = jnp.where((klane >= lo) & (klane < lo + SEL_BLK), seen, pen)
        s_g.append(_dot(qb[:, g * HEAD_DIM:(g + 1) * HEAD_DIM], kt_g) + pen)
    grp0_k = _mod(lax.broadcasted_iota(jnp.int32, (N_HEADS, n_keys), 0), KV_HEADS) == 0
    s = jnp.where(grp0_k, s_g[0], s_g[1])
    r_new = ns_ref[0].astype(BF16).astype(F32)
    k_new, v_new = r_new[:, :KV_LANES], r_new[:, KV_LANES:]
    s_new = jnp.sum(qe * k_new, axis=-1, keepdims=True)
    m = jnp.maximum(jnp.max(s, axis=-1, keepdims=True), s_new)
    p = jnp.exp(s - m)
    p_new = jnp.exp(s_new - m)
    den = jnp.sum(p, axis=-1, keepdims=True) + p_new
    pb = p.astype(BF16)
    pv = []
    for g in range(KV_HEADS):
        vt_g = jnp.concatenate([buf_ref[slot, g * TOP_N + k, 1] for k in range(TOP_N)], axis=1).astype(BF16)
        pv.append(_dot_nt(pb, vt_g))
    o_sel = (jnp.concatenate(pv, axis=1) + p_new.astype(BF16).astype(F32) * v_new) / den

    w_len = win_ref.shape[2]
    nw = nwt_ref[0]
    pick = lax.broadcasted_iota(jnp.int32, nw.shape, 1) == b
    new_col = jnp.sum(jnp.where(pick, nw, 0.0), axis=-1, keepdims=True)
    wlane = lax.broadcasted_iota(jnp.int32, (KV_ROW, w_len), 1)
    wout = jnp.where(wlane == w_len - 1, new_col, pltpu.roll(win_ref[0], w_len - 1, axis=1))
    wout_ref[0] = wout
    sw = _dot(qb, wout[:KV_LANES].astype(BF16))
    ew = jnp.exp(sw - jnp.max(sw, axis=-1, keepdims=True))
    o_win = _dot_nt(ew.astype(BF16), wout[KV_LANES:].astype(BF16)) / jnp.sum(ew, axis=-1, keepdims=True)

    gn = jnp.broadcast_to(gn_ref[0], (N_HEADS, LANES))
    head = _mod(row, KV_HEADS) * HPG + _div(row, KV_HEADS)

    def gate(branch):
        return jnp.sum(jnp.where(lane == head * N_BRANCH + branch, gn, 0.0), axis=-1, keepdims=True)

    out = gate(0) * ocmp_ref[0] + gate(1) * o_sel + gate(2) * o_win
    o_ref[0] = jnp.concatenate(
        [jnp.sum(jnp.where((_div(row, KV_HEADS) == v) & (_div(lane, HEAD_DIM) == _mod(row, KV_HEADS)), out, 0.0),
                 axis=0, keepdims=True)
         for v in range(HPG)], axis=1).astype(BF16)


def _sample_attn(page_table, idx, cache, q, gn, new_sel, new_win_t, win_t, ocmp, *, t_pos):
    bsz, n_pages = page_table.shape
    page = cache.shape[-1]
    w_len = win_t.shape[2]
    one = lambda n: pl.BlockSpec((1, 1, n), lambda i, *_: (i, 0, 0))
    win_spec = pl.BlockSpec((1, KV_ROW, w_len), lambda i, *_: (i, 0, 0))
    gs = pltpu.PrefetchScalarGridSpec(
        num_scalar_prefetch=2,
        grid=(bsz,),
        in_specs=[pl.BlockSpec(memory_space=pl.ANY), one(D_ATTN), one(LANES), one(KV_ROW),
                  pl.BlockSpec(new_win_t.shape, lambda i, *_: (0, 0, 0), pipeline_mode=pl.Buffered(1)), win_spec,
                  pl.BlockSpec((1, N_HEADS, LANES), lambda i, *_: (i, 0, 0))],
        out_specs=[one(D_ATTN), win_spec],
        scratch_shapes=[pltpu.VMEM((2, KV_HEADS * TOP_N, 2, HEAD_DIM, page), F32), pltpu.SemaphoreType.DMA((2,))],
    )
    return pl.pallas_call(
        functools.partial(_sample_attn_body, n_pages=n_pages, page=page, t_pos=t_pos),
        grid_spec=gs,
        out_shape=[jax.ShapeDtypeStruct((bsz, 1, D_ATTN), BF16), jax.ShapeDtypeStruct(win_t.shape, F32)],
        compiler_params=_cparams(1),
        name="sample_attn",
    )(page_table.reshape(-1), idx.reshape(-1), cache, q, gn, new_sel, new_win_t, win_t, ocmp)


def _rope_tables(pos):
    half = HEAD_DIM // 2
    freqs = jnp.power(ROPE_THETA, -jnp.arange(half, dtype=F32) * (2.0 / HEAD_DIM))
    ang = pos.astype(F32)[:, None] * freqs[None, :]
    cos, sin = jnp.cos(ang), jnp.sin(ang)
    reps = LANES // HEAD_DIM
    return (jnp.tile(jnp.concatenate([cos, cos], axis=1), (1, reps)),
            jnp.tile(jnp.concatenate([-sin, sin], axis=1), (1, reps)), cos.T, sin.T)


def _compress_weights(w1k, w2k, pek, w1v, w2v, pev):
    flat = CMP_STRIDE * KV_LANES

    def pe_rows(pe):
        both = jnp.broadcast_to(pe.reshape(CMP_RATIO, CMP_STRIDE, 1, HEAD_DIM),
                                (CMP_RATIO, CMP_STRIDE, KV_HEADS, HEAD_DIM))
        return jnp.pad(both.reshape(CMP_RATIO, flat), ((0, 8 - CMP_RATIO), (0, 0)))

    def first(w1):
        base = w1.transpose(1, 2, 0, 3)
        full = jnp.zeros((CMP_STRIDE, KV_HEADS, HEAD_DIM, KV_HEADS, CMP_RATIO, CMP_HID), w1.dtype)
        for g in range(KV_HEADS):
            full = full.at[:, g, :, g].set(base)
        return full.reshape(flat, KV_HEADS * CMP_RATIO * CMP_HID)

    def pair(w2):
        z = jnp.zeros_like(w2)
        return jnp.concatenate([jnp.concatenate([w2, z], axis=1), jnp.concatenate([z, w2], axis=1)],
                               axis=0).astype(BF16)

    return (jnp.stack([pe_rows(pek), pe_rows(pev)]).astype(BF16), jnp.stack([first(w1k), first(w1v)]).astype(BF16),
            pair(w2k), pair(w2v))


def _cmp_to_sel(n_cmp_pad, n_sel_pad, n_cmp, n_sel):
    i = np.arange(n_cmp_pad)[:, None]
    j = np.arange(n_sel_pad)[None, :]
    start = i * CMP_STRIDE
    return (start < j * SEL_BLK + SEL_BLK) & (start + CMP_BLK > j * SEL_BLK) & (i < n_cmp) & (j < n_sel)


def _kv_out(rows_t):
    n, _, t = rows_t.shape
    return rows_t.reshape(1, n, 2, KV_HEADS, HEAD_DIM, t).transpose(0, 1, 5, 2, 3, 4)


def _kv_in(a):
    n, t = a.shape[:2]
    return a.transpose(0, 2, 3, 4, 1).reshape(n, 2, KV_LANES, t)


def kernel(x_prompt, x_sample, cache_cmp_kv, cache_sel_kv, state_win_kv, state_conv, page_table, norm_ffn1, ffn1_w_gate, ffn1_w_up, ffn1_w_down, norm_mix, w_mix_in, conv_w, w_cmpk1, w_cmpk2, pe_cmpk, w_cmpv1, w_cmpv2, pe_cmpv, w_conv_out, w_attn_out, w_mix_out, norm_ffn2, ffn2_w_gate, ffn2_w_up, ffn2_w_down, norm_final):
    depth = norm_ffn1.shape[0]
    assert depth == 1
    bsz, t_len, _ = x_prompt.shape
    dbsz, dseq, _ = x_sample.shape
    n_pages = page_table.shape[1]
    page = cache_cmp_kv.shape[2]
    past = n_pages * page
    w_len = state_win_kv.shape[2]
    assert dseq == 1 and w_len == WINDOW and past >= WINDOW and past % SEL_BLK == 0 and page % SEL_BLK == 0
    assert t_len >= WINDOW and t_len // CMP_STRIDE == LANES

    offs = np.cumsum(IN_SPLITS)[:-1].tolist()
    w_ub, w_uc, w_ux, w_q, w_kvc, w_kvs, w_kvw, w_gn, w_gb = jnp.split(w_mix_in[0], offs, axis=1)
    order = np.asarray(HEAD_ORDER)
    w_q = w_q.reshape(D_MODEL, N_HEADS, HEAD_DIM)[:, order].reshape(D_MODEL, D_ATTN)
    w_gn = jnp.pad(w_gn, ((0, 0), (0, LANES - w_gn.shape[1])))
    w_r = jnp.concatenate([w_uc, w_ux, w_q, w_gn], axis=1).astype(BF16)
    w_r_s = jnp.concatenate([w_uc, w_ux, w_q, w_gn, w_kvs, w_kvw], axis=1).astype(BF16)
    w_t = jnp.concatenate([w_kvc, w_kvs, w_kvw], axis=1).T.astype(BF16)
    w_b = jnp.concatenate([w_ub, w_gb], axis=1).astype(BF16)
    w_ao = w_attn_out[0].reshape(N_HEADS, HEAD_DIM, D_MODEL)[order].reshape(D_ATTN, D_MODEL).astype(BF16)
    w_co = w_conv_out[0].astype(BF16)
    w_mo = w_mix_out[0].astype(BF16)
    cw = _compress_weights(w_cmpk1[0], w_cmpk2[0], pe_cmpk[0], w_cmpv1[0], w_cmpv2[0], pe_cmpv[0])
    ffn1 = (norm_ffn1, ffn1_w_gate[0].astype(BF16), ffn1_w_up[0].astype(BF16), ffn1_w_down[0].astype(BF16))
    ffn2 = (norm_ffn2, ffn2_w_gate[0].astype(BF16), ffn2_w_up[0].astype(BF16), ffn2_w_down[0].astype(BF16))
    g_final = norm_final.reshape(1, D_MODEL)

    xp = _ffn(x_prompt.reshape(bsz * t_len, D_MODEL), *ffn1, g_final, final_norm=False)
    tabs = _rope_tables(jnp.arange(t_len, dtype=jnp.int32))
    uin, q, gn, rct, rst, rwt, kst, kwt = _mixin(xp.reshape(bsz, t_len, D_MODEL), norm_mix, w_r, w_t, tabs,
                                                 rows_out=False)
    kct, vc = _compress_prompt(rct, cw)
    n_chunks = t_len // CMP_STRIDE
    n_sel = t_len // SEL_BLK
    e_t = jnp.asarray(np.arange(LANES)[:, None] == np.arange(t_len)[None, :] // SEL_BLK, BF16)
    c2s_t = jnp.asarray(_cmp_to_sel(n_chunks, LANES, n_chunks - CMP_RATIO + 1, n_sel).T, BF16)
    o = _nsa_prompt(q, gn, kct, vc, kst, kwt, e_t, c2s_t)
    uin = uin.reshape(bsz * t_len, D_CONV)
    xp = _merge(xp, norm_mix, uin, uin, uin, o.reshape(bsz * t_len, D_ATTN), w_b, conv_w[0], w_co, w_ao, w_mo,
                seq_len=t_len)
    y_prompt = _ffn(xp, *ffn2, g_final, final_norm=True).reshape(bsz, t_len, D_MODEL)
    cmp_p = _kv_out(rct)
    sel_p = _kv_out(rst)
    win_p = _kv_out(rwt[:, :, t_len - WINDOW:])
    conv_p = uin.reshape(1, bsz, t_len, D_CONV)[:, :, t_len - (CONV_W - 1):]

    xs = _ffn(x_sample.reshape(dbsz, D_MODEL), *ffn1, g_final, final_norm=False)
    tabs_s = _rope_tables(jnp.full((dbsz,), past, jnp.int32))
    uin_s, q_s, gn_s, rct_s, rst_s, rwt_s, _, _, rs_s, _ = _mixin(xs.reshape(1, dbsz, D_MODEL), norm_mix, w_r_s,
                                                                  w_t, tabs_s, rows_out=True)
    one = lambda a: a.reshape(dbsz, 1, a.shape[-1])
    n_chunks_s = past // CMP_STRIDE
    n_sel_s = past // SEL_BLK + 1
    n_sel_pad = -(-n_sel_s // LANES) * LANES
    c2s_s = jnp.asarray(_cmp_to_sel(n_chunks_s, n_sel_pad, (past + 1) // CMP_STRIDE - CMP_RATIO + 1, n_sel_s), BF16)
    ocmp, idx = _sample_cmp(page_table, _kv_in(cache_cmp_kv[0]), one(q_s), cw, c2s_s, t_pos=past)
    idx = idx[:, :KV_HEADS, :TOP_N]
    win_t = _kv_in(state_win_kv[0]).reshape(dbsz, KV_ROW, w_len)
    o_s, win_s = _sample_attn(page_table, idx, _kv_in(cache_sel_kv[0]), one(q_s), one(gn_s), one(rs_s), rwt_s,
                              win_t, ocmp, t_pos=past)
    uin_s = uin_s.reshape(dbsz, D_CONV)
    xs = _merge(xs, norm_mix, uin_s, state_conv[0, :, 1], state_conv[0, :, 0], o_s.reshape(dbsz, D_ATTN), w_b,
                conv_w[0], w_co, w_ao, w_mo, seq_len=None)
    y_sample = _ffn(xs, *ffn2, g_final, final_norm=True).reshape(dbsz, 1, D_MODEL)
    new_out = lambda rt: rt.reshape(1, 2, KV_HEADS, HEAD_DIM, dbsz).transpose(0, 4, 1, 2, 3)[:, :, None]
    cmp_s = new_out(rct_s)
    sel_s = new_out(rst_s)
    win_s = _kv_out(win_s)
    conv_s = jnp.concatenate([state_conv[0, :, 1:], uin_s[:, None, :]], axis=1)[None]
    return (y_prompt, y_sample, cmp_p, sel_p, win_p, conv_p, cmp_s, sel_s, win_s, conv_s)
```

```python
import functools

import numpy as np
import jax
import jax.numpy as jnp
from jax import lax
from jax.experimental import pallas as pl
from jax.experimental.pallas import tpu as pltpu

F32 = jnp.float32
BF16 = jnp.bfloat16

D_MODEL = 1024
D_CONV = 512
CONV_W = 3
N_HEADS = 8
KV_HEADS = 2
HPG = N_HEADS // KV_HEADS
HEAD_DIM = 64
D_ATTN = N_HEADS * HEAD_DIM
KV_ROW = 2 * KV_HEADS * HEAD_DIM
CMP_STRIDE = 16
CMP_RATIO = 2
CMP_BLK = CMP_STRIDE * CMP_RATIO
CMP_HID = 128
SEL_BLK = 64
TOP_N = 16
WINDOW = 512
N_BRANCH = 3
D_FF = 2816
ROPE_THETA = 10000.0
EPS = 1e-6
NEG = -1e30
BIG = 1e9
GONE = -3e38
IN_SPLITS = (D_CONV, D_CONV, D_CONV, D_ATTN, KV_ROW, KV_ROW, KV_ROW, N_HEADS * N_BRANCH, 2 * D_MODEL)

LANES = 128
KV_LANES = KV_HEADS * HEAD_DIM
assert KV_LANES == LANES and KV_HEADS == 2 and 2 * HEAD_DIM == LANES
HEAD_ORDER = tuple(half * HPG + v for v in range(HPG) for half in range(KV_HEADS))
VMEM_LIMIT = 56 * 1024 * 1024
ROW_TILE = 512
KEY_TILE = 256


def _cparams(n_grid):
    return pltpu.CompilerParams(dimension_semantics=("arbitrary",) * n_grid, vmem_limit_bytes=VMEM_LIMIT)


def _const_spec(shape):
    nd = len(shape)
    return pl.BlockSpec(shape, lambda *_: (0,) * nd, pipeline_mode=pl.Buffered(1))


def _rms(x, g):
    return x * lax.rsqrt(jnp.mean(x * x, axis=-1, keepdims=True) + EPS) * g


def _silu(x):
    return x * jax.nn.sigmoid(x)


def _div(x, n):
    return lax.shift_right_logical(x, int(np.log2(n)))


def _mod(x, n):
    return x & (n - 1)


def _dot(a, b):
    return jnp.dot(a, b, preferred_element_type=F32)


def _dot_nt(a, b):
    return lax.dot_general(a, b, (((1,), (1,)), ((), ())), preferred_element_type=F32)


def _ffn_body(x_ref, g_ref, wg_ref, wu_ref, wd_ref, gf_ref, o_ref, *, final_norm):
    x = x_ref[...]
    h = _rms(x, g_ref[...]).astype(BF16)
    a = (_silu(_dot(h, wg_ref[...])) * _dot(h, wu_ref[...])).astype(BF16)
    y = x + 0.5 * _dot(a, wd_ref[...])
    if final_norm:
        y = _rms(y, gf_ref[...])
    o_ref[...] = y


def _ffn(x, g, wg, wu, wd, gf, *, final_norm):
    m = x.shape[0]
    tm = min(ROW_TILE, m)
    row = pl.BlockSpec((tm, D_MODEL), lambda i: (i, 0))
    return pl.pallas_call(
        functools.partial(_ffn_body, final_norm=final_norm),
        grid=(m // tm,),
        in_specs=[row, _const_spec((1, D_MODEL)), _const_spec(wg.shape), _const_spec(wu.shape),
                  _const_spec(wd.shape), _const_spec((1, D_MODEL))],
        out_specs=row,
        out_shape=jax.ShapeDtypeStruct((m, D_MODEL), F32),
        compiler_params=_cparams(1),
        name="ffn",
    )(x, g, wg, wu, wd, gf)


ROW_COLS = {"uc": (0, 512), "ux": (512, 1024), "q": (1024, 1536), "gn": (1536, 1664), "kvs": (1664, 1920),
            "kvw": (1920, 2176)}


def _rope_tile(z, cos, sin):
    first = _mod(lax.broadcasted_iota(jnp.int32, z.shape, 1), HEAD_DIM) < (HEAD_DIM // 2)
    rot = jnp.where(first, pltpu.roll(z, LANES - HEAD_DIM // 2, axis=1), pltpu.roll(z, HEAD_DIM // 2, axis=1))
    return z * cos + rot * sin


def _rope_rows_t(zt, cos_t, sin_t):
    half = HEAD_DIM // 2
    out = []
    for hh in range(KV_LANES // HEAD_DIM):
        x1 = zt[hh * HEAD_DIM:hh * HEAD_DIM + half]
        x2 = zt[hh * HEAD_DIM + half:(hh + 1) * HEAD_DIM]
        out += [x1 * cos_t - x2 * sin_t, x1 * sin_t + x2 * cos_t]
    return jnp.concatenate(out, axis=0)


def _mixin_body(x_ref, g_ref, wr_ref, wt_ref, cos_ref, sin_ref, cost_ref, sint_ref,
                uin_ref, q_ref, gn_ref, rct_ref, rst_ref, rwt_ref, kst_ref, kwt_ref, *row_refs):
    h = _rms(x_ref[0], g_ref[...]).astype(BF16)
    cos = cos_ref[...]
    sin = sin_ref[...]

    def proj(name):
        lo, hi = ROW_COLS[name]
        return _dot(h, wr_ref[:, lo:hi])

    uin_ref[0] = proj("uc") * proj("ux")
    zq = proj("q")
    scale = HEAD_DIM ** -0.5
    q_ref[0] = jnp.concatenate(
        [_rope_tile(zq[:, v * LANES:(v + 1) * LANES], cos, sin) * scale for v in range(D_ATTN // LANES)],
        axis=1).astype(BF16)
    gn_ref[0] = jax.nn.sigmoid(proj("gn"))
    cos_t = cost_ref[...]
    sin_t = sint_ref[...]
    for i, (f_ref, b_ref) in enumerate(((rct_ref, None), (rst_ref, kst_ref), (rwt_ref, kwt_ref))):
        zt = _dot_nt(wt_ref[i * KV_ROW:(i + 1) * KV_ROW, :], h)
        rows_t = jnp.concatenate([_rope_rows_t(zt[:KV_LANES], cos_t, sin_t), zt[KV_LANES:]], axis=0)
        f_ref[0] = rows_t
        if b_ref is not None:
            b_ref[0] = rows_t.astype(BF16)
    for name, r_ref in zip(("kvs", "kvw"), row_refs):
        z = proj(name)
        r_ref[0] = jnp.concatenate([_rope_tile(z[:, :KV_LANES], cos, sin), z[:, KV_LANES:]], axis=1)


def _mixin(x, g, wr, wt, tabs, *, rows_out):
    b, t, _ = x.shape
    tm = min(2 * ROW_TILE, t)
    row = lambda n: pl.BlockSpec((1, tm, n), lambda i, j: (i, j, 0))
    col = pl.BlockSpec((1, KV_ROW, tm), lambda i, j: (i, 0, j))
    tab = pl.BlockSpec((tm, LANES), lambda i, j: (j, 0))
    tab_t = pl.BlockSpec((HEAD_DIM // 2, tm), lambda i, j: (0, j))
    rsd = lambda n, dt: jax.ShapeDtypeStruct((b, t, n), dt)
    csd = lambda dt: jax.ShapeDtypeStruct((b, KV_ROW, t), dt)
    extra = 2 if rows_out else 0
    return pl.pallas_call(
        _mixin_body,
        grid=(b, t // tm),
        in_specs=[row(D_MODEL), _const_spec((1, D_MODEL)), _const_spec(wr.shape), _const_spec(wt.shape),
                  tab, tab, tab_t, tab_t],
        out_specs=[row(D_CONV), row(D_ATTN), row(LANES), col, col, col, col, col] + [row(KV_ROW)] * extra,
        out_shape=[rsd(D_CONV, F32), rsd(D_ATTN, BF16), rsd(LANES, F32), csd(F32), csd(F32), csd(F32), csd(BF16),
                   csd(BF16)] + [rsd(KV_ROW, F32)] * extra,
        compiler_params=_cparams(2),
        name="mix_in",
    )(x, g, wr, wt, *tabs)


CHUNK_PITCH = 24


def _store_chunks(rows_ref, kv, first_chunk, rows):
    for j in range(rows.shape[0] // CMP_STRIDE):
        lo = (first_chunk + j) * CHUNK_PITCH
        rows_ref[kv, lo:lo + CMP_STRIDE, :] = rows[j * CMP_STRIDE:(j + 1) * CMP_STRIDE]


def _compress(rows_ref, n_chunks, pef_ref, w1_ref, w2k_ref, w2v_ref):
    keep = lax.broadcasted_iota(jnp.int32, (n_chunks, LANES), 0) < n_chunks - 1
    out = []
    for kv, w2_ref in enumerate((w2k_ref, w2v_ref)):
        lhs = jnp.concatenate([rows_ref[kv, pl.ds(s, n_chunks, stride=CHUNK_PITCH), :].astype(BF16)
                               for s in range(CMP_STRIDE)], axis=1)
        part = _dot(lhs, w1_ref[kv])
        pe_part = _dot(pef_ref[kv], w1_ref[kv])
        hid = []
        for g in range(KV_HEADS):
            c0, c1 = (CMP_RATIO * g) * CMP_HID, (CMP_RATIO * g + 1) * CMP_HID
            bias = pe_part[0:1, c0:c0 + CMP_HID] + pe_part[1:2, c1:c1 + CMP_HID]
            hid.append(part[:, c0:c0 + CMP_HID] + pltpu.roll(part[:, c1:c1 + CMP_HID], n_chunks - 1, axis=0) + bias)
        act = _silu(jnp.concatenate(hid, axis=1)).astype(BF16)
        out.append(jnp.where(keep, _dot(act, w2_ref[...]), 0.0))
    return out


def _compress_prompt_body(xt_ref, pef_ref, w1_ref, w2k_ref, w2v_ref, kct_ref, vc_ref, rows_ref,
                          *, n_chunks, t_len):
    for kv in range(2):
        for c in range(t_len // LANES):
            _store_chunks(rows_ref, kv, c * (LANES // CMP_STRIDE),
                          xt_ref[0, kv * KV_LANES:(kv + 1) * KV_LANES, c * LANES:(c + 1) * LANES].T)
    kc, vc = _compress(rows_ref, n_chunks, pef_ref, w1_ref, w2k_ref, w2v_ref)
    kct_ref[0] = kc.T.astype(BF16)
    vc_ref[0] = vc.astype(BF16)


def _compress_prompt(rct, cw):
    b, _, t = rct.shape
    n_chunks = t // CMP_STRIDE
    out = pl.BlockSpec((1, n_chunks, LANES), lambda i: (i, 0, 0))
    return pl.pallas_call(
        functools.partial(_compress_prompt_body, n_chunks=n_chunks, t_len=t),
        grid=(b,),
        in_specs=[pl.BlockSpec((1, KV_ROW, t), lambda i: (i, 0, 0))] + [_const_spec(w.shape) for w in cw],
        out_specs=[out, out],
        out_shape=[jax.ShapeDtypeStruct((b, n_chunks, LANES), BF16)] * 2,
        scratch_shapes=[pltpu.VMEM((2, n_chunks * CHUNK_PITCH, LANES), F32)],
        compiler_params=_cparams(1),
        name="compress_prompt",
    )(rct, *cw)


def _softmax_rows(s):
    e = jnp.exp(s - jnp.max(s, axis=-1, keepdims=True))
    return e / jnp.sum(e, axis=-1, keepdims=True)


def _split_hi_lo(x):
    hi = x.astype(BF16)
    return hi, (x - hi.astype(F32)).astype(BF16)


def _nsa_prompt_body(q_ref, gn_ref, kct_ref, vc_ref, ks_ref, kw_ref, e_ref, c2st_ref, o_ref,
                     qaug_ref, p_ref, m_ref, l_ref, acc_ref, alpha_ref, out_ref, *, tq, kt, t_len):
    n_slots = N_HEADS
    rows = n_slots * tq
    n_sel = t_len // SEL_BLK
    start = pl.program_id(1) * tq
    lane = lax.broadcasted_iota(jnp.int32, (tq, LANES), 1)
    t_pos = start + lax.broadcasted_iota(jnp.int32, (tq, LANES), 0)
    low = lane < HEAD_DIM
    gn = gn_ref[0]
    slot_rows = [slice(slot * tq, (slot + 1) * tq) for slot in range(n_slots)]

    def gate(slot, branch):
        v, half = divmod(slot, KV_HEADS)
        c = (half * HPG + v) * N_BRANCH + branch
        return gn[:, c:c + 1]

    q = q_ref[0].astype(F32)
    for slot, rs in enumerate(slot_rows):
        v, half = divmod(slot, KV_HEADS)
        qv = q[:, v * LANES:(v + 1) * LANES]
        qaug_ref[rs, :LANES] = (jnp.where(low, qv, 0.0) if half == 0 else jnp.where(low, 0.0, qv)).astype(BF16)

    cmp_ok = lane * CMP_STRIDE + (CMP_BLK - 1) <= t_pos
    s_c = _dot(qaug_ref[:, :LANES], kct_ref[0])
    p_c = jnp.concatenate(
        [jnp.where(cmp_ok, _softmax_rows(jnp.where(cmp_ok, s_c[rs], NEG)), 0.0) for rs in slot_rows], axis=0)
    o_c = _dot(p_c.astype(BF16), vc_ref[0])
    for slot, rs in enumerate(slot_rows):
        out_ref[rs, :] = gate(slot, 0) * o_c[rs]

    few_blocks = start + tq <= TOP_N * SEL_BLK

    @pl.when(few_blocks)
    def _():
        qaug_ref[:, LANES:] = jnp.zeros((rows, LANES), BF16)

    @pl.when(jnp.logical_not(few_blocks))
    def _():
        blk = lax.broadcasted_iota(jnp.int32, (n_sel, tq), 0)
        cur = _div(start + lax.broadcasted_iota(jnp.int32, (n_sel, tq), 1), SEL_BLK)
        valid = blk <= cur
        forced = valid & ((blk == 0) | (blk == cur) | (blk == cur - 1))
        for half in range(KV_HEADS):
            psum = p_c[slot_rows[half]]
            for v in range(1, HPG):
                psum = psum + p_c[slot_rows[v * KV_HEADS + half]]
            hi, lo = _split_hi_lo(psum)
            imp = (_dot_nt(c2st_ref[...], hi) + _dot_nt(c2st_ref[...], lo))[:n_sel]
            score = jnp.where(forced, BIG, jnp.where(valid, imp, -BIG))
            rank = jnp.zeros((n_sel, tq), F32)
            for i in range(n_sel):
                r = score[i:i + 1, :]
                tie = jnp.where(blk > i, 1.0, 0.0)
                rank = rank + jnp.where(r > score, 1.0, jnp.where(r == score, tie, 0.0))
            pen_t = jnp.concatenate([jnp.where(rank >= TOP_N, NEG, 0.0), jnp.zeros((LANES - n_sel, tq), F32)],
                                    axis=0)
            pen = pen_t.T.astype(BF16)
            for v in range(HPG):
                qaug_ref[slot_rows[v * KV_HEADS + half], LANES:] = pen

    def reset():
        m_ref[...] = jnp.full((rows, LANES), NEG, F32)
        l_ref[...] = jnp.zeros((rows, LANES), F32)
        acc_ref[...] = jnp.zeros((rows, LANES), F32)

    d = lax.broadcasted_iota(jnp.int32, (tq, kt), 1) - lax.broadcasted_iota(jnp.int32, (tq, kt), 0)
    visible = {"causal": d <= 0, "lower": d > 0}

    def tile_step(kv_ref, koff, select, masks, only_step=False):
        w = len(masks) * kt
        kv = kv_ref[0, :, pl.ds(koff, w)]
        if select:
            s_all = _dot(qaug_ref[...], jnp.concatenate([kv[:KV_LANES], e_ref[:, pl.ds(koff, w)]], axis=0))
        else:
            s_all = _dot(qaug_ref[:, :LANES], kv[:KV_LANES])
        for rs in slot_rows:
            s = jnp.concatenate(
                [s_all[rs, jb * kt:(jb + 1) * kt] if mask is None
                 else jnp.where(visible[mask], s_all[rs, jb * kt:(jb + 1) * kt], NEG)
                 for jb, mask in enumerate(masks)], axis=1)
            if only_step:
                m_new = jnp.broadcast_to(jnp.max(s, axis=-1, keepdims=True), (tq, LANES))
            else:
                m_prev = m_ref[rs, :]
                m_new = jnp.maximum(m_prev, jnp.max(s, axis=-1, keepdims=True))
                alpha_ref[rs, :] = jnp.exp(m_prev - m_new)
                m_ref[rs, :] = m_new
            p_ref[rs, :w] = jnp.exp(s - jnp.concatenate([m_new] * (w // LANES), axis=1)).astype(BF16)
        pv = _dot_nt(p_ref[:, :w], jnp.concatenate([kv[KV_LANES:], jnp.ones((KV_LANES, w), BF16)], axis=0))
        if only_step:
            return pv
        acc_ref[...] = alpha_ref[...] * acc_ref[...] + pv[:, :LANES]
        l_ref[...] = alpha_ref[...] * l_ref[...] + pv[:, LANES:]

    def add_branch(branch, acc, den):
        for slot, rs in enumerate(slot_rows):
            out_ref[rs, :] = out_ref[rs, :] + gate(slot, branch) * (acc[rs, :] / den[rs, :])

    reset()
    n_full = start // kt

    def sel_body(j, carry):
        tile_step(ks_ref, pl.multiple_of(j * 2 * kt, 2 * kt), True, (None, None))
        return carry

    lax.fori_loop(0, n_full // 2, sel_body, 0)

    @pl.when(n_full % 2 == 1)
    def _():
        tile_step(ks_ref, pl.multiple_of(start - kt, kt), True, (None, "causal"))

    @pl.when(n_full % 2 == 0)
    def _():
        tile_step(ks_ref, pl.multiple_of(start, kt), True, ("causal",))
    add_branch(1, acc_ref, l_ref)

    n_win = WINDOW // kt
    for n_before in range(n_win + 1):
        masks = [None] * n_before + ["causal"]
        if n_before == n_win:
            masks[0] = "lower"

        @pl.when(start >= WINDOW if n_before == n_win else start == n_before * kt)
        def _():
            pv = tile_step(kw_ref, pl.multiple_of(start - n_before * kt, kt), False, masks, only_step=True)
            add_branch(2, pv[:, :LANES], pv[:, LANES:])

    o_ref[0] = jnp.concatenate(
        [jnp.where(low, out_ref[slot_rows[2 * v], :], out_ref[slot_rows[2 * v + 1], :]) for v in range(HPG)],
        axis=1).astype(BF16)


def _nsa_prompt(q, gn, kct, vc, kst, kwt, e_t, c2s_t):
    b, t, _ = q.shape
    tq = kt = KEY_TILE
    assert WINDOW % kt == 0 and t % kt == 0
    n_cmp = vc.shape[1]
    rows = N_HEADS * tq
    tile = lambda n: pl.BlockSpec((1, tq, n), lambda i, j: (i, j, 0))
    per_b = lambda r, n: pl.BlockSpec((1, r, n), lambda i, j: (i, 0, 0))
    rows_f32 = pltpu.VMEM((rows, LANES), F32)
    return pl.pallas_call(
        functools.partial(_nsa_prompt_body, tq=tq, kt=kt, t_len=t),
        grid=(b, t // tq),
        in_specs=[tile(D_ATTN), tile(LANES), per_b(LANES, n_cmp), per_b(n_cmp, LANES), per_b(KV_ROW, t),
                  per_b(KV_ROW, t), _const_spec(e_t.shape), _const_spec(c2s_t.shape)],
        out_specs=tile(D_ATTN),
        out_shape=jax.ShapeDtypeStruct((b, t, D_ATTN), BF16),
        scratch_shapes=[pltpu.VMEM((rows, 2 * LANES), BF16), pltpu.VMEM((rows, max(2 * kt, WINDOW + kt)), BF16), rows_f32, rows_f32,
                        rows_f32, rows_f32, rows_f32],
        compiler_params=_cparams(2),
        name="nsa_prompt",
    )(q, gn, kct, vc, kst, kwt, e_t, c2s_t)


def _merge_body(x_ref, g_ref, u_ref, um1_ref, um2_ref, o_ref, wb_ref, cw_ref, wco_ref, wao_ref, wmo_ref,
                y_ref, *, tm, tiles_per_seq):
    x = x_ref[...]
    h = _rms(x, g_ref[...]).astype(BF16)
    ub = _dot(h, wb_ref[:, :D_CONV])
    u = u_ref[...]
    if tiles_per_seq is None:
        um1 = um1_ref[...]
        um2 = um2_ref[...]
    else:
        row = lax.broadcasted_iota(jnp.int32, (tm, D_CONV), 0)
        keep = jnp.where(pl.program_id(0) % tiles_per_seq == 0, 0.0, 1.0)
        prev = um1_ref[...] * keep
        um1 = jnp.where(row == 0, prev[7:8, :], pltpu.roll(u, 1, axis=0))
        um2 = jnp.where(row == 0, prev[6:7, :], jnp.where(row == 1, prev[7:8, :], pltpu.roll(u, 2, axis=0)))
    conv = cw_ref[0:1, :] * um2 + cw_ref[1:2, :] * um1 + cw_ref[2:3, :] * u
    y_c = _dot((ub * conv).astype(BF16), wco_ref[...])
    y_a = _dot(o_ref[...], wao_ref[...])
    g_c = jax.nn.sigmoid(_dot(h, wb_ref[:, D_CONV:D_CONV + D_MODEL]))
    g_a = jax.nn.sigmoid(_dot(h, wb_ref[:, D_CONV + D_MODEL:]))
    y_ref[...] = x + _dot((g_c * y_c + g_a * y_a).astype(BF16), wmo_ref[...])


def _merge(x, g, uin, um1, um2, o, wb, cw, wco, wao, wmo, *, seq_len):
    m = x.shape[0]
    tm = min(2 * ROW_TILE, m)
    row = lambda n: pl.BlockSpec((tm, n), lambda i: (i, 0))
    if seq_len is None:
        prev_specs = [row(D_CONV), row(D_CONV)]
        tiles_per_seq = None
    else:
        halo = pl.BlockSpec((8, D_CONV), lambda i: (jnp.maximum(i * (tm // 8) - 1, 0), 0))
        prev_specs = [halo, halo]
        tiles_per_seq = seq_len // tm
    return pl.pallas_call(
        functools.partial(_merge_body, tm=tm, tiles_per_seq=tiles_per_seq),
        grid=(m // tm,),
        in_specs=[row(D_MODEL), _const_spec((1, D_MODEL)), row(D_CONV)] + prev_specs +
                 [row(D_ATTN), _const_spec(wb.shape), _const_spec(cw.shape), _const_spec(wco.shape),
                  _const_spec(wao.shape), _const_spec(wmo.shape)],
        out_specs=row(D_MODEL),
        out_shape=jax.ShapeDtypeStruct((m, D_MODEL), F32),
        compiler_params=_cparams(1),
        name="merge",
    )(x, g, uin, um1, um2, o, wb, cw, wco, wao, wmo)


def _expand_q(q_ref):
    row = lax.broadcasted_iota(jnp.int32, (N_HEADS, LANES), 0)
    lane = lax.broadcasted_iota(jnp.int32, (N_HEADS, LANES), 1)
    q = q_ref[0].astype(F32)
    qe = jnp.zeros((N_HEADS, LANES), F32)
    for v in range(HPG):
        qv = jnp.broadcast_to(q[:, v * LANES:(v + 1) * LANES], (N_HEADS, LANES))
        qe = jnp.where((_div(row, KV_HEADS) == v) & (_div(lane, HEAD_DIM) == _mod(row, KV_HEADS)), qv, qe)
    return qe


def _sample_cmp_body(pt_ref, cache_ref, q_ref, pef_ref, w1_ref, w2k_ref, w2v_ref, c2s_ref,
                     ocmp_ref, idx_ref, buf_ref, rows_ref, sem_ref, *, n_pages, page, t_pos):
    b = pl.program_id(0)
    n_b = pl.num_programs(0)
    n_chunks = n_pages * page // CMP_STRIDE
    n_sel = t_pos // SEL_BLK + 1

    def page_copy(bb, p, slot, wait=False):
        src = 0 if wait else pt_ref[bb * n_pages + p]
        return pltpu.make_async_copy(cache_ref.at[src], buf_ref.at[slot, p], sem_ref.at[slot])

    def start_all(bb, slot):
        for p in range(n_pages):
            page_copy(bb, p, slot).start()

    @pl.when(b == 0)
    def _():
        start_all(0, 0)

    @pl.when(b + 1 < n_b)
    def _():
        start_all(b + 1, (b + 1) % 2)

    slot = b % 2
    for p in range(n_pages):
        page_copy(b, p, slot, wait=True).wait()

    group = 8
    for p0 in range(0, n_pages, group):
        for kv in range(2):
            wide = jnp.concatenate([buf_ref[slot, p0 + i, kv] for i in range(group)], axis=1)
            _store_chunks(rows_ref, kv, p0 * page // CMP_STRIDE, wide.T)
    kc, vc = _compress(rows_ref, n_chunks, pef_ref, w1_ref, w2k_ref, w2v_ref)

    lane = lax.broadcasted_iota(jnp.int32, (N_HEADS, LANES), 1)
    qe = _expand_q(q_ref)
    cidx = lax.broadcasted_iota(jnp.int32, (N_HEADS, n_chunks), 1)
    cmp_ok = cidx * CMP_STRIDE + (CMP_BLK - 1) <= t_pos
    p = jnp.where(cmp_ok, _softmax_rows(jnp.where(cmp_ok, _dot_nt(qe.astype(BF16), kc.astype(BF16)), NEG)), 0.0)
    ocmp_ref[0] = _dot(p.astype(BF16), vc.astype(BF16))

    hi, lo = _split_hi_lo(p)
    imp_h = _dot(hi, c2s_ref[...]) + _dot(lo, c2s_ref[...])
    n_lanes = c2s_ref.shape[1]
    grow = lax.broadcasted_iota(jnp.int32, (N_HEADS, n_lanes), 0)
    glane = lax.broadcasted_iota(jnp.int32, (N_HEADS, n_lanes), 1)
    imp = jnp.zeros((N_HEADS, n_lanes), F32)
    for g in range(KV_HEADS):
        tot = jnp.sum(jnp.where(_mod(grow, KV_HEADS) == g, imp_h, 0.0), axis=0, keepdims=True)
        imp = jnp.where(grow == g, jnp.broadcast_to(tot, imp.shape), imp)
    cur = t_pos // SEL_BLK
    forced = (glane == 0) | (glane == cur) | (glane == cur - 1)
    score = jnp.where(glane < n_sel, jnp.where(forced, BIG, imp), GONE)
    rank = jnp.zeros((N_HEADS, n_lanes), F32)
    for i in range(n_sel):
        col = score[:, i:i + 1]
        tie = jnp.where(glane > i, 1.0, 0.0)
        rank = rank + jnp.where(col > score, 1.0, jnp.where(col == score, tie, 0.0))
    flane = glane.astype(F32)
    picked = jnp.zeros((N_HEADS, LANES), jnp.int32)
    for k in range(TOP_N):
        block_k = jnp.sum(jnp.where(rank == float(k), flane, 0.0), axis=-1, keepdims=True)
        picked = jnp.where(lane == k, block_k.astype(jnp.int32), picked)
    idx_ref[0] = picked


def _sample_cmp(page_table, cache, q, cw, c2s, *, t_pos):
    bsz, n_pages = page_table.shape
    page = cache.shape[-1]
    assert page == LANES
    gs = pltpu.PrefetchScalarGridSpec(
        num_scalar_prefetch=1,
        grid=(bsz,),
        in_specs=[pl.BlockSpec(memory_space=pl.ANY), pl.BlockSpec((1, 1, D_ATTN), lambda i, pt: (i, 0, 0))] +
                 [pl.BlockSpec(w.shape, (lambda nd: lambda i, pt: (0,) * nd)(w.ndim), pipeline_mode=pl.Buffered(1))
                  for w in (*cw, c2s)],
        out_specs=[pl.BlockSpec((1, N_HEADS, LANES), lambda i, pt: (i, 0, 0))] * 2,
        scratch_shapes=[pltpu.VMEM((2, n_pages, 2, KV_LANES, page), F32),
                        pltpu.VMEM((2, n_pages * page // CMP_STRIDE * CHUNK_PITCH, KV_LANES), F32),
                        pltpu.SemaphoreType.DMA((2,))],
    )
    return pl.pallas_call(
        functools.partial(_sample_cmp_body, n_pages=n_pages, page=page, t_pos=t_pos),
        grid_spec=gs,
        out_shape=[jax.ShapeDtypeStruct((bsz, N_HEADS, LANES), F32),
                   jax.ShapeDtypeStruct((bsz, N_HEADS, LANES), jnp.int32)],
        compiler_params=_cparams(1),
        name="sample_cmp",
    )(page_table.reshape(-1), cache, q, *cw, c2s)


def _sample_attn_body(pt_ref, idx_ref, cache_ref, q_ref, gn_ref, ns_ref, nwt_ref, win_ref, ocmp_ref,
                      o_ref, wout_ref, buf_ref, sem_ref, *, n_pages, page, t_pos):
    b = pl.program_id(0)
    n_b = pl.num_programs(0)
    halves = page // SEL_BLK
    new_blk = t_pos // SEL_BLK
    n_slot = KV_HEADS * TOP_N

    def blk_of(bb, k):
        return idx_ref[bb * n_slot + k]

    def page_copy(bb, k, slot, wait=False):
        j = 0 if wait else jnp.minimum(blk_of(bb, k), new_blk - 1)
        src = 0 if wait else pt_ref[bb * n_pages + _div(j, halves)]
        feats = pl.ds((k // TOP_N) * HEAD_DIM, HEAD_DIM)
        return pltpu.make_async_copy(cache_ref.at[src, :, feats, :], buf_ref.at[slot, k], sem_ref.at[slot])

    def start_all(bb, slot):
        for k in range(n_slot):
            page_copy(bb, k, slot).start()

    @pl.when(b == 0)
    def _():
        start_all(0, 0)

    @pl.when(b + 1 < n_b)
    def _():
        start_all(b + 1, (b + 1) % 2)

    slot = b % 2
    for k in range(n_slot):
        page_copy(b, k, slot, wait=True).wait()

    row = lax.broadcasted_iota(jnp.int32, (N_HEADS, LANES), 0)
    lane = lax.broadcasted_iota(jnp.int32, (N_HEADS, LANES), 1)
    grp0 = _mod(row, KV_HEADS) == 0
    qe = _expand_q(q_ref)
    qb = qe.astype(BF16)

    n_keys = TOP_N * page
    klane = lax.broadcasted_iota(jnp.int32, (N_HEADS, n_keys), 1)
    s_g = []
    for g in range(KV_HEADS):
        kt_g = jnp.concatenate([buf_ref[slot, g * TOP_N + k, 0] for k in range(TOP_N)], axis=1).astype(BF16)
        pen = jnp.full((N_HEADS, n_keys), NEG, F32)
        for k in range(TOP_N):
            j = blk_of(b, g * TOP_N + k)
            lo = k * page + _mod(j, halves) * SEL_BLK
            seen = jnp.where(j >= new_blk, NEG, 0.0)
            pen = jnp.where((klane >= lo) & (klane < lo + SEL_BLK), seen, pen)
        s_g.append(_dot(qb[:, g * HEAD_DIM:(g + 1) * HEAD_DIM], kt_g) + pen)
    grp0_k = _mod(lax.broadcasted_iota(jnp.int32, (N_HEADS, n_keys), 0), KV_HEADS) == 0
    s = jnp.where(grp0_k, s_g[0], s_g[1])
    r_new = ns_ref[0].astype(BF16).astype(F32)
    k_new, v_new = r_new[:, :KV_LANES], r_new[:, KV_LANES:]
    s_new = jnp.sum(qe * k_new, axis=-1, keepdims=True)
    m = jnp.maximum(jnp.max(s, axis=-1, keepdims=True), s_new)
    p = jnp.exp(s - m)
    p_new = jnp.exp(s_new - m)
    den = jnp.sum(p, axis=-1, keepdims=True) + p_new
    pb = p.astype(BF16)
    pv = []
    for g in range(KV_HEADS):
        vt_g = jnp.concatenate([buf_ref[slot, g * TOP_N + k, 1] for k in range(TOP_N)], axis=1).astype(BF16)
        pv.append(_dot_nt(pb, vt_g))
    o_sel = (jnp.concatenate(pv, axis=1) + p_new.astype(BF16).astype(F32) * v_new) / den

    w_len = win_ref.shape[2]
    nw = nwt_ref[0]
    pick = lax.broadcasted_iota(jnp.int32, nw.shape, 1) == b
    new_col = jnp.sum(jnp.where(pick, nw, 0.0), axis=-1, keepdims=True)
    wlane = lax.broadcasted_iota(jnp.int32, (KV_ROW, w_len), 1)
    wout = jnp.where(wlane == w_len - 1, new_col, pltpu.roll(win_ref[0], w_len - 1, axis=1))
    wout_ref[0] = wout
    sw = _dot(qb, wout[:KV_LANES].astype(BF16))
    ew = jnp.exp(sw - jnp.max(sw, axis=-1, keepdims=True))
    o_win = _dot_nt(ew.astype(BF16), wout[KV_LANES:].astype(BF16)) / jnp.sum(ew, axis=-1, keepdims=True)

    gn = jnp.broadcast_to(gn_ref[0], (N_HEADS, LANES))
    head = _mod(row, KV_HEADS) * HPG + _div(row, KV_HEADS)

    def gate(branch):
        return jnp.sum(jnp.where(lane == head * N_BRANCH + branch, gn, 0.0), axis=-1, keepdims=True)

    out = gate(0) * ocmp_ref[0] + gate(1) * o_sel + gate(2) * o_win
    o_ref[0] = jnp.concatenate(
        [jnp.sum(jnp.where((_div(row, KV_HEADS) == v) & (_div(lane, HEAD_DIM) == _mod(row, KV_HEADS)), out, 0.0),
                 axis=0, keepdims=True)
         for v in range(HPG)], axis=1).astype(BF16)


def _sample_attn(page_table, idx, cache, q, gn, new_sel, new_win_t, win_t, ocmp, *, t_pos):
    bsz, n_pages = page_table.shape
    page = cache.shape[-1]
    w_len = win_t.shape[2]
    one = lambda n: pl.BlockSpec((1, 1, n), lambda i, *_: (i, 0, 0))
    win_spec = pl.BlockSpec((1, KV_ROW, w_len), lambda i, *_: (i, 0, 0))
    gs = pltpu.PrefetchScalarGridSpec(
        num_scalar_prefetch=2,
        grid=(bsz,),
        in_specs=[pl.BlockSpec(memory_space=pl.ANY), one(D_ATTN), one(LANES), one(KV_ROW),
                  pl.BlockSpec(new_win_t.shape, lambda i, *_: (0, 0, 0), pipeline_mode=pl.Buffered(1)), win_spec,
                  pl.BlockSpec((1, N_HEADS, LANES), lambda i, *_: (i, 0, 0))],
        out_specs=[one(D_ATTN), win_spec],
        scratch_shapes=[pltpu.VMEM((2, KV_HEADS * TOP_N, 2, HEAD_DIM, page), F32), pltpu.SemaphoreType.DMA((2,))],
    )
    return pl.pallas_call(
        functools.partial(_sample_attn_body, n_pages=n_pages, page=page, t_pos=t_pos),
        grid_spec=gs,
        out_shape=[jax.ShapeDtypeStruct((bsz, 1, D_ATTN), BF16), jax.ShapeDtypeStruct(win_t.shape, F32)],
        compiler_params=_cparams(1),
        name="sample_attn",
    )(page_table.reshape(-1), idx.reshape(-1), cache, q, gn, new_sel, new_win_t, win_t, ocmp)


def _rope_tables(pos):
    half = HEAD_DIM // 2
    freqs = jnp.power(ROPE_THETA, -jnp.arange(half, dtype=F32) * (2.0 / HEAD_DIM))
    ang = pos.astype(F32)[:, None] * freqs[None, :]
    cos, sin = jnp.cos(ang), jnp.sin(ang)
    reps = LANES // HEAD_DIM
    return (jnp.tile(jnp.concatenate([cos, cos], axis=1), (1, reps)),
            jnp.tile(jnp.concatenate([-sin, sin], axis=1), (1, reps)), cos.T, sin.T)


def _compress_weights(w1k, w2k, pek, w1v, w2v, pev):
    flat = CMP_STRIDE * KV_LANES

    def pe_rows(pe):
        both = jnp.broadcast_to(pe.reshape(CMP_RATIO, CMP_STRIDE, 1, HEAD_DIM),
                                (CMP_RATIO, CMP_STRIDE, KV_HEADS, HEAD_DIM))
        return jnp.pad(both.reshape(CMP_RATIO, flat), ((0, 8 - CMP_RATIO), (0, 0)))

    def first(w1):
        base = w1.transpose(1, 2, 0, 3)
        full = jnp.zeros((CMP_STRIDE, KV_HEADS, HEAD_DIM, KV_HEADS, CMP_RATIO, CMP_HID), w1.dtype)
        for g in range(KV_HEADS):
            full = full.at[:, g, :, g].set(base)
        return full.reshape(flat, KV_HEADS * CMP_RATIO * CMP_HID)

    def pair(w2):
        z = jnp.zeros_like(w2)
        return jnp.concatenate([jnp.concatenate([w2, z], axis=1), jnp.concatenate([z, w2], axis=1)],
                               axis=0).astype(BF16)

    return (jnp.stack([pe_rows(pek), pe_rows(pev)]).astype(BF16), jnp.stack([first(w1k), first(w1v)]).astype(BF16),
            pair(w2k), pair(w2v))


def _cmp_to_sel(n_cmp_pad, n_sel_pad, n_cmp, n_sel):
    i = np.arange(n_cmp_pad)[:, None]
    j = np.arange(n_sel_pad)[None, :]
    start = i * CMP_STRIDE
    return (start < j * SEL_BLK + SEL_BLK) & (start + CMP_BLK > j * SEL_BLK) & (i < n_cmp) & (j < n_sel)


def _kv_out(rows_t):
    n, _, t = rows_t.shape
    return rows_t.reshape(1, n, 2, KV_HEADS, HEAD_DIM, t).transpose(0, 1, 5, 2, 3, 4)


def _kv_in(a):
    n, t = a.shape[:2]
    return a.transpose(0, 2, 3, 4, 1).reshape(n, 2, KV_LANES, t)


def kernel(x_prompt, x_sample, cache_cmp_kv, cache_sel_kv, state_win_kv, state_conv, page_table, norm_ffn1, ffn1_w_gate, ffn1_w_up, ffn1_w_down, norm_mix, w_mix_in, conv_w, w_cmpk1, w_cmpk2, pe_cmpk, w_cmpv1, w_cmpv2, pe_cmpv, w_conv_out, w_attn_out, w_mix_out, norm_ffn2, ffn2_w_gate, ffn2_w_up, ffn2_w_down, norm_final):
    depth = norm_ffn1.shape[0]
    assert depth == 1
    bsz, t_len, _ = x_prompt.shape
    dbsz, dseq, _ = x_sample.shape
    n_pages = page_table.shape[1]
    page = cache_cmp_kv.shape[2]
    past = n_pages * page
    w_len = state_win_kv.shape[2]
    assert dseq == 1 and w_len == WINDOW and past >= WINDOW and past % SEL_BLK == 0 and page % SEL_BLK == 0
    assert t_len >= WINDOW and t_len // CMP_STRIDE == LANES

    offs = np.cumsum(IN_SPLITS)[:-1].tolist()
    w_ub, w_uc, w_ux, w_q, w_kvc, w_kvs, w_kvw, w_gn, w_gb = jnp.split(w_mix_in[0], offs, axis=1)
    order = np.asarray(HEAD_ORDER)
    w_q = w_q.reshape(D_MODEL, N_HEADS, HEAD_DIM)[:, order].reshape(D_MODEL, D_ATTN)
    w_gn = jnp.pad(w_gn, ((0, 0), (0, LANES - w_gn.shape[1])))
    w_r = jnp.concatenate([w_uc, w_ux, w_q, w_gn], axis=1).astype(BF16)
    w_r_s = jnp.concatenate([w_uc, w_ux, w_q, w_gn, w_kvs, w_kvw], axis=1).astype(BF16)
    w_t = jnp.concatenate([w_kvc, w_kvs, w_kvw], axis=1).T.astype(BF16)
    w_b = jnp.concatenate([w_ub, w_gb], axis=1).astype(BF16)
    w_ao = w_attn_out[0].reshape(N_HEADS, HEAD_DIM, D_MODEL)[order].reshape(D_ATTN, D_MODEL).astype(BF16)
    w_co = w_conv_out[0].astype(BF16)
    w_mo = w_mix_out[0].astype(BF16)
    cw = _compress_weights(w_cmpk1[0], w_cmpk2[0], pe_cmpk[0], w_cmpv1[0], w_cmpv2[0], pe_cmpv[0])
    ffn1 = (norm_ffn1, ffn1_w_gate[0].astype(BF16), ffn1_w_up[0].astype(BF16), ffn1_w_down[0].astype(BF16))
    ffn2 = (norm_ffn2, ffn2_w_gate[0].astype(BF16), ffn2_w_up[0].astype(BF16), ffn2_w_down[0].astype(BF16))
    g_final = norm_final.reshape(1, D_MODEL)

    xp = _ffn(x_prompt.reshape(bsz * t_len, D_MODEL), *ffn1, g_final, final_norm=False)
    tabs = _rope_tables(jnp.arange(t_len, dtype=jnp.int32))
    uin, q, gn, rct, rst, rwt, kst, kwt = _mixin(xp.reshape(bsz, t_len, D_MODEL), norm_mix, w_r, w_t, tabs,
                                                 rows_out=False)
    kct, vc = _compress_prompt(rct, cw)
    n_chunks = t_len // CMP_STRIDE
    n_sel = t_len // SEL_BLK
    e_t = jnp.asarray(np.arange(LANES)[:, None] == np.arange(t_len)[None, :] // SEL_BLK, BF16)
    c2s_t = jnp.asarray(_cmp_to_sel(n_chunks, LANES, n_chunks - CMP_RATIO + 1, n_sel).T, BF16)
    o = _nsa_prompt(q, gn, kct, vc, kst, kwt, e_t, c2s_t)
    uin = uin.reshape(bsz * t_len, D_CONV)
    xp = _merge(xp, norm_mix, uin, uin, uin, o.reshape(bsz * t_len, D_ATTN), w_b, conv_w[0], w_co, w_ao, w_mo,
                seq_len=t_len)
    y_prompt = _ffn(xp, *ffn2, g_final, final_norm=True).reshape(bsz, t_len, D_MODEL)
    cmp_p = _kv_out(rct)
    sel_p = _kv_out(rst)
    win_p = _kv_out(rwt[:, :, t_len - WINDOW:])
    conv_p = uin.reshape(1, bsz, t_len, D_CONV)[:, :, t_len - (CONV_W - 1):]

    xs = _ffn(x_sample.reshape(dbsz, D_MODEL), *ffn1, g_final, final_norm=False)
    tabs_s = _rope_tables(jnp.full((dbsz,), past, jnp.int32))
    uin_s, q_s, gn_s, rct_s, rst_s, rwt_s, _, _, rs_s, _ = _mixin(xs.reshape(1, dbsz, D_MODEL), norm_mix, w_r_s,
                                                                  w_t, tabs_s, rows_out=True)
    one = lambda a: a.reshape(dbsz, 1, a.shape[-1])
    n_chunks_s = past // CMP_STRIDE
    n_sel_s = past // SEL_BLK + 1
    n_sel_pad = -(-n_sel_s // LANES) * LANES
    c2s_s = jnp.asarray(_cmp_to_sel(n_chunks_s, n_sel_pad, (past + 1) // CMP_STRIDE - CMP_RATIO + 1, n_sel_s), BF16)
    ocmp, idx = _sample_cmp(page_table, _kv_in(cache_cmp_kv[0]), one(q_s), cw, c2s_s, t_pos=past)
    idx = idx[:, :KV_HEADS, :TOP_N]
    win_t = _kv_in(state_win_kv[0]).reshape(dbsz, KV_ROW, w_len)
    o_s, win_s = _sample_attn(page_table, idx, _kv_in(cache_sel_kv[0]), one(q_s), one(gn_s), one(rs_s), rwt_s,
                              win_t, ocmp, t_pos=past)
    uin_s = uin_s.reshape(dbsz, D_CONV)
    xs = _merge(xs, norm_mix, uin_s, state_conv[0, :, 1], state_conv[0, :, 0], o_s.reshape(dbsz, D_ATTN), w_b,
                conv_w[0], w_co, w_ao, w_mo, seq_len=None)
    y_sample = _ffn(xs, *ffn2, g_final, final_norm=True).reshape(dbsz, 1, D_MODEL)
    new_out = lambda rt: rt.reshape(1, 2, KV_HEADS, HEAD_DIM, dbsz).transpose(0, 4, 1, 2, 3)[:, :, None]
    cmp_s = new_out(rct_s)
    sel_s = new_out(rst_s)
    win_s = _kv_out(win_s)
    conv_s = jnp.concatenate([state_conv[0, :, 1:], uin_s[:, None, :]], axis=1)[None]
    return (y_prompt, y_sample, cmp_p, sel_p, win_p, conv_p, cmp_s, sel_s, win_s, conv_s)
```
